```python
import jax
import jax.numpy as jnp
from jax import lax
import numpy as np

D_MODEL = 2048
BATCH = 16
SEQ = 256
DEPTH = 2
DEC_BATCH = 8
DEC_SEQ = 2048
PAST_LEN = 512

GRID_W = 64
N_HEADS = 8
N_KV_HEADS = 2
HEAD_DIM = 128
ATT_Q_W = N_HEADS * HEAD_DIM
ATT_KV_W = N_KV_HEADS * HEAD_DIM
Q_BLOCK = 128
ROPE_THETA = 10000.0
CHUNK = 128
CM_HEADS = 4
CM_DIM = 128
CM_W = CM_HEADS * CM_DIM
CV_W = 512
CONV_WIDTH = 3
D_MIX = ATT_Q_W + CM_W + CV_W
D_IN = ATT_Q_W + 2 * ATT_KV_W + 2 * CM_W + 3 * CV_W
PEER_HEADS = 8
N_KEYS = 128
N_EXPERTS = N_KEYS * N_KEYS
PEER_TOPK = 16
PQ_DIM = 256
PK_HALF = PQ_DIM // 2
PEER_BLOCK = 128
N_MOD = 6
EPS = 1e-6

kernel_name = "hybrid_dit_prefix_peer_step"


def _rms(x, g=None):
    xf = x.astype(jnp.float32)
    y = xf * lax.rsqrt(jnp.mean(xf * xf, axis=-1, keepdims=True) + EPS)
    if g is not None:
        y = y * g.astype(jnp.float32)
    return y.astype(x.dtype)


def axial_rope_tables(n_tokens):
    rows = n_tokens // GRID_W
    t = jnp.arange(rows * GRID_W)
    pos = jnp.stack([t // GRID_W, t % GRID_W], axis=-1).astype(jnp.float32)
    n_freq = HEAD_DIM // 4
    inv = ROPE_THETA ** (-jnp.arange(n_freq, dtype=jnp.float32) / n_freq)
    ang = pos[:, :, None] * inv
    ang = jnp.broadcast_to(ang[:, :, None, :], (ang.shape[0], 2, 2, n_freq)).reshape(-1, HEAD_DIM)
    return jnp.cos(ang), jnp.sin(ang)


def apply_rope(x, cos, sin):
    xs = x.reshape(x.shape[:-1] + (2, 2, HEAD_DIM // 4))
    rot = jnp.stack([-xs[..., 1, :], xs[..., 0, :]], axis=-2).reshape(x.shape)
    c = cos[None, :, None, :]
    s = sin[None, :, None, :]
    return (x.astype(jnp.float32) * c + rot.astype(jnp.float32) * s).astype(x.dtype)


def block_attention(q, k, v):
    B, Lq, H, hd = q.shape
    nb = Lq // Q_BLOCK
    G = H // N_KV_HEADS
    scale = HEAD_DIM ** -0.5
    qb = q.reshape(B, nb, Q_BLOCK, N_KV_HEADS, G, hd).transpose(1, 0, 2, 3, 4, 5)

    def one(qblk):
        s = jnp.einsum('bqkgd,bskd->bkgqs', qblk, k).astype(jnp.float32) * scale
        p = jax.nn.softmax(s, axis=-1).astype(v.dtype)
        return jnp.einsum('bkgqs,bskd->bqkgd', p, v)

    o = lax.map(one, qb)
    return o.transpose(1, 0, 2, 3, 4, 5).reshape(B, Lq, H * hd)


def chunk_mlp(u, v, w_s, b_s):
    B, L, _ = u.shape
    n = L // CHUNK
    vh = _rms(v.reshape(B, n, CHUNK, CM_HEADS, CM_DIM))
    mixed = jnp.einsum('hpq,bnqhc->bnphc', w_s, vh) + b_s.T[None, None, :, :, None]
    return (u.reshape(B, n, CHUNK, CM_HEADS, CM_DIM) * mixed).reshape(B, L, CM_W)


def short_conv(b_gate, c_gate, h, w):
    z = c_gate * h
    zp = jnp.pad(z, ((0, 0), (1, 1), (0, 0)))
    y = zp[:, :-2] * w[0] + zp[:, 1:-1] * w[1] + zp[:, 2:] * w[2]
    return b_gate * y


def peer_ffn(x, w_pq, sub_keys, u_tab, v_tab):
    B, L, D = x.shape
    xt = x.reshape((B * L) // PEER_BLOCK, PEER_BLOCK, D)

    def one(xb):
        q = (xb @ w_pq).reshape(PEER_BLOCK, PEER_HEADS, 2, PK_HALF)
        s = jnp.einsum('thpd,hpkd->thpk', q, sub_keys).astype(jnp.float32)
        sv, si = lax.top_k(s, PEER_TOPK)
        comb = (sv[:, :, 0, :, None] + sv[:, :, 1, None, :]).reshape(PEER_BLOCK, PEER_HEADS, PEER_TOPK * PEER_TOPK)
        bv, bi = lax.top_k(comb, PEER_TOPK)
        i1 = jnp.take_along_axis(si[:, :, 0], bi // PEER_TOPK, axis=-1)
        i2 = jnp.take_along_axis(si[:, :, 1], bi % PEER_TOPK, axis=-1)
        e = i1 * N_KEYS + i2
        g = jax.nn.softmax(bv, axis=-1)
        a = jax.nn.gelu(jnp.einsum('td,thkd->thk', xb, u_tab[e]).astype(jnp.float32))
        w = (g * a).astype(xb.dtype)
        return jnp.einsum('thk,thkd->td', w, v_tab[e])

    return lax.map(one, xt).reshape(B, L, D)


def run_layer(x, cond, p, ctx_kv=None, rope=None):
    B, L, _ = x.shape
    mod = (jax.nn.silu(cond) @ p['w_ada'] + p['b_ada']).reshape(B, N_MOD, D_MODEL)[:, :, None, :]
    sh1, sc1, g1, sh2, sc2, g2 = (mod[:, i] for i in range(N_MOD))

    h = _rms(x, p['g1']) * (1 + sc1) + sh1
    z = h @ p['w_in']
    o1 = ATT_Q_W
    o2 = o1 + ATT_KV_W
    o3 = o2 + ATT_KV_W
    o4 = o3 + CM_W
    o5 = o4 + CM_W
    o6 = o5 + CV_W
    o7 = o6 + CV_W
    q, k, v, u_cm, v_cm, b_cv, c_cv, h_cv = jnp.split(z, [o1, o2, o3, o4, o5, o6, o7], axis=-1)

    q = _rms(q.reshape(B, L, N_HEADS, HEAD_DIM), p['q_gain'])
    k = _rms(k.reshape(B, L, N_KV_HEADS, HEAD_DIM), p['k_gain'])
    v = v.reshape(B, L, N_KV_HEADS, HEAD_DIM)
    if ctx_kv is None:
        keys, vals = k, v
    else:
        cos, sin = rope
        q = apply_rope(q, cos, sin)
        keys = jnp.concatenate([apply_rope(k, cos, sin), ctx_kv[0]], axis=1)
        vals = jnp.concatenate([v, ctx_kv[1]], axis=1)
    attn = block_attention(q, keys, vals)
    cm = chunk_mlp(u_cm, v_cm, p['w_s'], p['b_s'])
    cv = short_conv(b_cv, c_cv, h_cv, p['conv_w'])
    mix = jnp.concatenate([attn, cm, cv], axis=-1) @ p['w_out']
    x = x + g1 * mix

    h2 = _rms(x, p['g2']) * (1 + sc2) + sh2
    x = x + g2 * peer_ffn(h2, p['w_pq'], p['sub_keys'], p['u_tab'], p['v_tab'])
    return x, k, v


def setup_inputs(seed: int = 0) -> dict:
    key = jax.random.key(seed)
    ks = jax.random.split(key, 24)
    f32 = jnp.float32
    nrm = lambda k, shape, s: jax.random.normal(k, shape, f32) * s
    return {
        "x_prompt": nrm(ks[0], (BATCH, SEQ, D_MODEL), 1.0),
        "x_sample": nrm(ks[1], (DEC_BATCH, DEC_SEQ, D_MODEL), 1.0),
        "cache_k": nrm(ks[2], (DEC_BATCH, DEPTH, PAST_LEN, N_KV_HEADS, HEAD_DIM), 1.0),
        "cache_v": nrm(ks[3], (DEC_BATCH, DEPTH, PAST_LEN, N_KV_HEADS, HEAD_DIM), 1.0),
        "c": nrm(ks[4], (DEC_BATCH, D_MODEL), 1.0),
        "c_ctx": nrm(ks[5], (D_MODEL,), 1.0),
        "w_ada": nrm(ks[6], (DEPTH, D_MODEL, N_MOD * D_MODEL), 0.5 * D_MODEL ** -0.5),
        "b_ada": nrm(ks[7], (DEPTH, N_MOD * D_MODEL), 0.01),
        "g_norm1": 1.0 + nrm(ks[8], (DEPTH, D_MODEL), 0.01),
        "g_norm2": 1.0 + nrm(ks[9], (DEPTH, D_MODEL), 0.01),
        "w_in": nrm(ks[10], (DEPTH, D_MODEL, D_IN), D_MODEL ** -0.5),
        "q_gain": 1.0 + nrm(ks[11], (DEPTH, HEAD_DIM), 0.01),
        "k_gain": 1.0 + nrm(ks[12], (DEPTH, HEAD_DIM), 0.01),
        "w_s": nrm(ks[13], (DEPTH, CM_HEADS, CHUNK, CHUNK), CHUNK ** -0.5),
        "b_s": 1.0 + nrm(ks[14], (DEPTH, CM_HEADS, CHUNK), 0.01),
        "conv_w": nrm(ks[15], (DEPTH, CONV_WIDTH, CV_W), CONV_WIDTH ** -0.5),
        "w_out": nrm(ks[16], (DEPTH, D_MIX, D_MODEL), D_MIX ** -0.5),
        "w_pq": nrm(ks[17], (DEPTH, D_MODEL, PEER_HEADS * PQ_DIM), D_MODEL ** -0.5),
        "sub_keys": nrm(ks[18], (DEPTH, PEER_HEADS, 2, N_KEYS, PK_HALF), PK_HALF ** -0.5),
        "u_tab": nrm(ks[19], (DEPTH, N_EXPERTS, D_MODEL), D_MODEL ** -0.5),
        "v_tab": nrm(ks[20], (DEPTH, N_EXPERTS, D_MODEL), 0.5),
        "g_final": 1.0 + nrm(ks[21], (D_MODEL,), 0.01),
    }


def reference(x_prompt, x_sample, cache_k, cache_v, c, c_ctx, w_ada, b_ada, g_norm1, g_norm2, w_in,
              q_gain, k_gain, w_s, b_s, conv_w, w_out, w_pq, sub_keys, u_tab, v_tab, g_final):
    cond_ctx = jnp.broadcast_to(c_ctx, (x_prompt.shape[0], D_MODEL))
    rope = axial_rope_tables(x_sample.shape[1])
    xc = x_prompt
    xs = x_sample
    ctx_keys = []
    ctx_vals = []
    for l in range(DEPTH):
        p = {
            'w_ada': w_ada[l], 'b_ada': b_ada[l], 'g1': g_norm1[l], 'g2': g_norm2[l],
            'w_in': w_in[l], 'q_gain': q_gain[l], 'k_gain': k_gain[l], 'w_s': w_s[l], 'b_s': b_s[l],
            'conv_w': conv_w[l], 'w_out': w_out[l], 'w_pq': w_pq[l], 'sub_keys': sub_keys[l],
            'u_tab': u_tab[l], 'v_tab': v_tab[l],
        }
        xc, kc, vc = run_layer(xc, cond_ctx, p)
        ctx_keys.append(kc)
        ctx_vals.append(vc)
        xs, _, _ = run_layer(xs, c, p, (cache_k[:, l], cache_v[:, l]), rope)
    y_prompt = _rms(xc, g_final)
    y_sample = _rms(xs, g_final)
    new_cache_k = jnp.stack(ctx_keys, axis=1)
    new_cache_v = jnp.stack(ctx_vals, axis=1)
    return (y_prompt, y_sample, new_cache_k, new_cache_v)
```

```python
import functools
import math

import jax
import jax.numpy as jnp
from jax import lax
from jax.experimental import pallas as pl
from jax.experimental.pallas import tpu as pltpu

F32 = jnp.float32
BF16 = jnp.bfloat16

D_MODEL = 2048
GRID_W = 64
N_HEADS = 8
N_KV = 2
HEAD_DIM = 128
GROUP = N_HEADS // N_KV
ATT_Q_W = N_HEADS * HEAD_DIM
ATT_KV_W = N_KV * HEAD_DIM
ROPE_THETA = 10000.0
CHUNK = 128
CM_HEADS = 4
CM_DIM = 128
CM_W = CM_HEADS * CM_DIM
CV_W = 512
D_IN = ATT_Q_W + 2 * ATT_KV_W + 2 * CM_W + 3 * CV_W
PEER_HEADS = 8
N_KEYS = 128
N_EXPERTS = N_KEYS * N_KEYS
TOPK = 16
PK_HALF = 128
N_GROUPS = 2 * PEER_HEADS
N_MOD = 6
EPS = 1e-6
NEG_INF = float("-inf")

LANES = 128
SUBLANES = 8
VMEM_LIMIT = 56 * 1024 * 1024

TM = 256
TQ = 128
TMP = 512
EC = 1024
ADA_NT = 1024
COND_ROWS = 16

O_Q = 0
O_K = O_Q + ATT_Q_W
O_V = O_K + ATT_KV_W
O_UCM = O_V + ATT_KV_W
O_VCM = O_UCM + CM_W
O_BCV = O_VCM + CM_W
O_CCV = O_BCV + CV_W
O_HCV = O_CCV + CV_W


def _dot(a, b):
    return jnp.dot(a, b, preferred_element_type=F32)


def _dot_nt(a, b):
    return lax.dot_general(a, b, (((1,), (1,)), ((), ())), preferred_element_type=F32)


def _split(x):
    hi = x.astype(BF16)
    lo = (x - hi.astype(F32)).astype(BF16)
    return hi, lo


def _rms_rows(x):
    return x * lax.rsqrt(jnp.mean(x * x, axis=-1, keepdims=True) + EPS)


def _params(sem):
    return pltpu.CompilerParams(dimension_semantics=sem, vmem_limit_bytes=VMEM_LIMIT)


def _ada_kernel(c_ref, w_ref, b_ref, o_ref):
    c = c_ref[...]
    a = c * jax.nn.sigmoid(c)
    a_hi, a_lo = _split(a)
    w_hi, w_lo = _split(w_ref[0])
    o_ref[0] = _dot(a_hi, w_hi) + _dot(a_lo, w_hi) + _dot(a_hi, w_lo) + b_ref[0]


def _ada(cond, w_ada, b_ada):
    depth, _, n_out = w_ada.shape
    return pl.pallas_call(
        _ada_kernel,
        grid=(depth, n_out // ADA_NT),
        in_specs=[
            pl.BlockSpec((COND_ROWS, D_MODEL), lambda l, j: (0, 0)),
            pl.BlockSpec((1, D_MODEL, ADA_NT), lambda l, j: (l, 0, j)),
            pl.BlockSpec((1, 1, ADA_NT), lambda l, j: (l, 0, j)),
        ],
        out_specs=pl.BlockSpec((1, COND_ROWS, ADA_NT), lambda l, j: (l, 0, j)),
        out_shape=jax.ShapeDtypeStruct((depth, COND_ROWS, n_out), F32),
        compiler_params=_params(("arbitrary", "arbitrary")),
        name="ada",
    )(cond, w_ada, b_ada.reshape(depth, 1, n_out))


def _inproj_kernel(*refs, has_ffn, n_ctx_tiles):
    refs = list(refs)
    x_ref = refs.pop(0)
    if has_ffn:
        ft_ref = refs.pop(0)
        modp_ref = refs.pop(0)
    (mod_ref, g1_ref, w_ref, qg_ref, kg_ref, cos_ref, sa_ref, sb_ref, ws_ref, bs_ref) = refs[:10]
    outs = refs[10:]
    if has_ffn:
        xo_ref = outs.pop(0)
    q_ref, kr_ref, vb_ref, kf_ref, vf_ref, cm_ref, bg_ref, zc_ref = outs

    i = pl.program_id(0)
    is_lat = i >= n_ctx_tiles
    x = x_ref[...]
    if has_ffn:
        x = x + modp_ref[0, 5:6, :] * ft_ref[...].T
        xo_ref[...] = x
    sh1 = mod_ref[0, 0:1, :]
    sc1 = mod_ref[0, 1:2, :]
    h = _rms_rows(x) * g1_ref[...]
    hb = (h * (1.0 + sc1) + sh1).astype(BF16)

    cos = jnp.where(is_lat, cos_ref[...], 1.0)
    sa = jnp.where(is_lat, sa_ref[...], 0.0)
    sb = jnp.where(is_lat, sb_ref[...], 0.0)

    def rope(y):
        return y * cos + pltpu.roll(y, 96, 1) * sa + pltpu.roll(y, 32, 1) * sb

    zq = _dot(hb, w_ref[:, O_Q:O_K])
    for hd in range(N_HEADS):
        sl = slice(hd * HEAD_DIM, (hd + 1) * HEAD_DIM)
        y = _rms_rows(zq[:, sl]) * qg_ref[...]
        q_ref[:, sl] = rope(y).astype(BF16)

    zk = _dot(hb, w_ref[:, O_K:O_V])
    zv = _dot(hb, w_ref[:, O_V:O_UCM])
    vb_ref[...] = zv.astype(BF16)
    kn = []
    for hd in range(N_KV):
        sl = slice(hd * HEAD_DIM, (hd + 1) * HEAD_DIM)
        y = _rms_rows(zk[:, sl]) * kg_ref[...]
        kn.append(y)
        kr_ref[:, sl] = rope(y).astype(BF16)

    @pl.when(jnp.logical_not(is_lat))
    def _():
        for hd in range(N_KV):
            kf_ref[:, hd * HEAD_DIM:(hd + 1) * HEAD_DIM] = kn[hd]
        vf_ref[...] = zv

    zu = _dot(hb, w_ref[:, O_UCM:O_VCM])
    zvc = _dot(hb, w_ref[:, O_VCM:O_BCV])
    n_chunks = zu.shape[0] // CHUNK
    for c in range(CM_HEADS):
        sl = slice(c * CM_DIM, (c + 1) * CM_DIM)
        vh = _rms_rows(zvc[:, sl]).astype(BF16)
        for n in range(n_chunks):
            rows = slice(n * CHUNK, (n + 1) * CHUNK)
            mixed = _dot(ws_ref[c], vh[rows]) + bs_ref[c]
            cm_ref[rows, sl] = (zu[rows, sl] * mixed).astype(BF16)

    bg_ref[...] = _dot(hb, w_ref[:, O_BCV:O_CCV])
    zc_ref[...] = _dot(hb, w_ref[:, O_CCV:O_HCV]) * _dot(hb, w_ref[:, O_HCV:D_IN])


def _inproj(x, ffn_t, mod_prev, mod, g1, w_in, q_gain, k_gain, cos, sa, sb, w_s, b_s, dims):
    n_tok, n_ctx, tps_lat = dims["n_tok"], dims["n_ctx"], dims["dec_seq"] // TM
    n_ctx_tiles = n_ctx // TM
    has_ffn = ffn_t is not None

    def mod_row(i):
        return jnp.where(i < n_ctx_tiles, 0, 1 + jnp.maximum(i - n_ctx_tiles, 0) // tps_lat)

    def pos_blk(i):
        return jnp.maximum(i - n_ctx_tiles, 0) % tps_lat

    def ctx_blk(i):
        return jnp.minimum(i, n_ctx_tiles - 1)

    tok = lambda w: pl.BlockSpec((TM, w), lambda i: (i, 0))
    full = lambda a: pl.BlockSpec(a.shape, lambda i: (0,) * a.ndim)
    mod_spec = pl.BlockSpec((1, N_MOD, D_MODEL), lambda i: (mod_row(i), 0, 0))
    rope_spec = pl.BlockSpec((TM, HEAD_DIM), lambda i: (pos_blk(i), 0))

    args, in_specs = [x], [tok(D_MODEL)]
    if has_ffn:
        args += [ffn_t, mod_prev]
        in_specs += [pl.BlockSpec((D_MODEL, TM), lambda i: (0, i)), mod_spec]
    args += [mod, g1, w_in, q_gain, k_gain, cos, sa, sb, w_s, b_s]
    in_specs += [mod_spec, full(g1), full(w_in), full(q_gain), full(k_gain),
                 rope_spec, rope_spec, rope_spec, full(w_s), full(b_s)]

    out_shape, out_specs = [], []
    if has_ffn:
        out_shape.append(jax.ShapeDtypeStruct((n_tok, D_MODEL), F32))
        out_specs.append(tok(D_MODEL))
    ctx_spec = pl.BlockSpec((TM, ATT_KV_W), lambda i: (ctx_blk(i), 0))
    out_shape += [
        jax.ShapeDtypeStruct((n_tok, ATT_Q_W), BF16),
        jax.ShapeDtypeStruct((n_tok, ATT_KV_W), BF16),
        jax.ShapeDtypeStruct((n_tok, ATT_KV_W), BF16),
        jax.ShapeDtypeStruct((n_ctx, ATT_KV_W), F32),
        jax.ShapeDtypeStruct((n_ctx, ATT_KV_W), F32),
        jax.ShapeDtypeStruct((n_tok, CM_W), BF16),
        jax.ShapeDtypeStruct((n_tok, CV_W), F32),
        jax.ShapeDtypeStruct((n_tok, CV_W), F32),
    ]
    out_specs += [tok(ATT_Q_W), tok(ATT_KV_W), tok(ATT_KV_W), ctx_spec, ctx_spec,
                  tok(CM_W), tok(CV_W), tok(CV_W)]
    return pl.pallas_call(
        functools.partial(_inproj_kernel, has_ffn=has_ffn, n_ctx_tiles=n_ctx_tiles),
        grid=(n_tok // TM,),
        in_specs=in_specs,
        out_specs=out_specs,
        out_shape=out_shape,
        compiler_params=_params(("arbitrary",)),
        name="inproj",
    )(*args)


def _attend(q_ref, parts, o_ref):
    t = q_ref.shape[0]
    scale = HEAD_DIM ** -0.5
    for kv in range(N_KV):
        ksl = slice(kv * HEAD_DIM, (kv + 1) * HEAD_DIM)
        qs = jnp.concatenate(
            [q_ref[:, (kv * GROUP + g) * HEAD_DIM:(kv * GROUP + g + 1) * HEAD_DIM] for g in range(GROUP)], axis=0)
        scores = [_dot_nt(qs, k_ref[:, ksl]) * scale for k_ref, _ in parts]
        m = scores[0].max(axis=-1, keepdims=True)
        for s in scores[1:]:
            m = jnp.maximum(m, s.max(axis=-1, keepdims=True))
        den = None
        num = None
        for s, (_, v_ref) in zip(scores, parts):
            p = jnp.exp(s - m)
            ps = p.sum(axis=-1, keepdims=True)
            pv = _dot(p.astype(BF16), v_ref[:, ksl])
            den = ps if den is None else den + ps
            num = pv if num is None else num + pv
        o = num / den
        for g in range(GROUP):
            hd = kv * GROUP + g
            o_ref[:, hd * HEAD_DIM:(hd + 1) * HEAD_DIM] = o[g * t:(g + 1) * t].astype(BF16)


def _attn_ctx_kernel(q_ref, k_ref, v_ref, o_ref):
    _attend(q_ref, [(k_ref, v_ref)], o_ref)


def _attn_lat_kernel(q_ref, k_ref, v_ref, ck_ref, cv_ref, prev_ref, o_ref):
    del prev_ref
    _attend(q_ref, [(k_ref, v_ref), (ck_ref, cv_ref)], o_ref)


def _attention(q, kr, vb, ck, cv, dims):
    n_tok, n_ctx = dims["n_tok"], dims["n_ctx"]
    batch, seq, dec_batch, dec_seq = dims["batch"], dims["seq"], dims["dec_batch"], dims["dec_seq"]
    past = ck.shape[1]
    attn = pl.pallas_call(
        _attn_ctx_kernel,
        grid=(batch,),
        in_specs=[
            pl.BlockSpec((seq, ATT_Q_W), lambda b: (b, 0)),
            pl.BlockSpec((seq, ATT_KV_W), lambda b: (b, 0)),
            pl.BlockSpec((seq, ATT_KV_W), lambda b: (b, 0)),
        ],
        out_specs=pl.BlockSpec((seq, ATT_Q_W), lambda b: (b, 0)),
        out_shape=jax.ShapeDtypeStruct((n_tok, ATT_Q_W), BF16),
        compiler_params=_params(("arbitrary",)),
        name="attn_ctx",
    )(q, kr, vb)
    nq = dec_seq // TQ
    q_off = n_ctx // TQ
    kv_off = n_ctx // dec_seq
    return pl.pallas_call(
        _attn_lat_kernel,
        grid=(dec_batch, nq),
        in_specs=[
            pl.BlockSpec((TQ, ATT_Q_W), lambda b, j: (q_off + b * nq + j, 0)),
            pl.BlockSpec((dec_seq, ATT_KV_W), lambda b, j: (kv_off + b, 0)),
            pl.BlockSpec((dec_seq, ATT_KV_W), lambda b, j: (kv_off + b, 0)),
            pl.BlockSpec((None, past, ATT_KV_W), lambda b, j: (b, 0, 0)),
            pl.BlockSpec((None, past, ATT_KV_W), lambda b, j: (b, 0, 0)),
            pl.BlockSpec(memory_space=pl.ANY),
        ],
        out_specs=pl.BlockSpec((TQ, ATT_Q_W), lambda b, j: (q_off + b * nq + j, 0)),
        out_shape=jax.ShapeDtypeStruct((n_tok, ATT_Q_W), BF16),
        input_output_aliases={5: 0},
        compiler_params=_params(("arbitrary", "arbitrary")),
        name="attn_lat",
    )(q, kr, vb, ck, cv, attn)


def _extract_top(vals, idxs, n):
    big = jnp.int32(1 << 30)
    maxima = []
    for _ in range(n):
        m = functools.reduce(jnp.maximum, vals).max(axis=0, keepdims=True)
        cand = [jnp.where(v == m, ix, big) for v, ix in zip(vals, idxs)]
        mi = functools.reduce(jnp.minimum, cand).min(axis=0, keepdims=True)
        vals = [jnp.where(ix == mi, NEG_INF, v) for v, ix in zip(vals, idxs)]
        maxima.append(m)
    return maxima, vals


def _pair_blocks(m1, m2, r8):
    vals = [m1[0:1] + m2[0:8], m1[0:1] + m2[8:16], m1[1:2] + m2[0:8]]
    idxs = [r8, r8 + 8, r8 + TOPK]
    for i in range(2, 8):
        vals.append(jnp.where(r8 < TOPK // (i + 1), m1[i:i + 1] + m2[0:8], NEG_INF))
        idxs.append(r8 + TOPK * i)
    vals.append(m1[8:16] + m2[0:1])
    idxs.append((r8 + 8) * TOPK)
    return vals, idxs


def _outproj_kernel(attn_ref, cm_ref, bg_ref, zc_ref, zp_ref, zn_ref, x_ref, mod_ref, wo_ref, cw_ref, g2_ref,
                    wph_ref, wpl_ref, sk_ref,
                    x1_ref, h2_ref, b1_ref, b2_ref, tau_ref,
                    zpad_scr, st_scr, m_scr, *, n_ctx_tiles, tps_ctx, tps_lat):
    i = pl.program_id(0)
    tm = x_ref.shape[0]
    is_lat = i >= n_ctx_tiles
    jj = jnp.where(is_lat, jnp.maximum(i - n_ctx_tiles, 0) % tps_lat, i % tps_ctx)
    nt = jnp.where(is_lat, tps_lat, tps_ctx)

    zc = zc_ref[...]
    zpad_scr[0:SUBLANES, :] = jnp.where(jj == 0, 0.0, zp_ref[...])
    zpad_scr[SUBLANES:SUBLANES + tm, :] = zc
    zpad_scr[SUBLANES + tm:2 * SUBLANES + tm, :] = jnp.where(jj == nt - 1, 0.0, zn_ref[...])
    zprev = zpad_scr[SUBLANES - 1:SUBLANES - 1 + tm, :]
    znext = zpad_scr[SUBLANES + 1:SUBLANES + 1 + tm, :]
    y = zprev * cw_ref[0:1, :] + zc * cw_ref[1:2, :] + znext * cw_ref[2:3, :]
    cv = (bg_ref[...] * y).astype(BF16)

    mix = _dot(jnp.concatenate([attn_ref[...], cm_ref[...], cv], axis=1), wo_ref[...])
    g1 = mod_ref[0, 2:3, :]
    sh2 = mod_ref[0, 3:4, :]
    sc2 = mod_ref[0, 4:5, :]
    x1 = x_ref[...] + g1 * mix
    x1_ref[...] = x1
    h2 = (_rms_rows(x1) * g2_ref[...]) * (1.0 + sc2) + sh2
    h_hi, h_lo = _split(h2)
    h2_ref[...] = h_hi

    qp = _dot(h_hi, wph_ref[...]) + _dot(h_lo, wph_ref[...]) + _dot(h_hi, wpl_ref[...])
    for g in range(N_GROUPS):
        q_hi, q_lo = _split(qp[:, g * PK_HALF:(g + 1) * PK_HALF])
        k_hi, k_lo = _split(sk_ref[g])
        st_scr[g] = _dot_nt(k_hi, q_hi) + _dot_nt(k_hi, q_lo) + _dot_nt(k_lo, q_hi)

    n_lt = tm // LANES

    def stage1(g, carry):
        for lt in range(n_lt):
            lanes = slice(lt * LANES, (lt + 1) * LANES)
            s = st_scr[g, :, lanes]
            rows = lax.broadcasted_iota(jnp.int32, s.shape, 0)
            maxima, left = _extract_top([s], [rows], TOPK)
            m_scr[g, :, lanes] = jnp.concatenate(maxima, axis=0)
            st_scr[g, :, lanes] = jnp.where(left[0] == s, NEG_INF, s)
        return carry

    lax.fori_loop(0, N_GROUPS, stage1, 0)

    def stage2(h, carry):
        for lt in range(n_lt):
            lanes = slice(lt * LANES, (lt + 1) * LANES)
            m1 = m_scr[2 * h, :, lanes]
            m2 = m_scr[2 * h + 1, :, lanes]
            top = m1[0:1] + m2[0:1]
            r8 = lax.broadcasted_iota(jnp.int32, (SUBLANES, LANES), 0)
            vals, idxs = _pair_blocks(m1, m2, r8)
            _, left = _extract_top(vals, idxs, TOPK)
            picked = [l != v for l, v in zip(left, vals)]
            z = functools.reduce(
                jnp.add, [jnp.where(p, jnp.exp(v - top), 0.0) for p, v in zip(picked, vals)]).sum(axis=0, keepdims=True)
            lz = jnp.log(z)
            shifted, _ = _pair_blocks((m1 - top) - lz, m2, r8)
            tau = functools.reduce(
                jnp.minimum, [jnp.where(p, v, jnp.inf) for p, v in zip(picked, shifted)]).min(axis=0, keepdims=True)
            tau_ref[h, :, lanes] = tau
            b1_ref[h, :, lanes] = (st_scr[2 * h, :, lanes] - top) - lz
            b2_ref[h, :, lanes] = st_scr[2 * h + 1, :, lanes]
        return carry

    lax.fori_loop(0, PEER_HEADS, stage2, 0)


def _outproj(attn, cm, bg, zc, x, mod, w_out, conv_w, g2, wp_hi, wp_lo, sub_keys, dims):
    n_tok, n_ctx = dims["n_tok"], dims["n_ctx"]
    n_ctx_tiles = n_ctx // TM
    n_tiles = n_tok // TM
    tps_lat = dims["dec_seq"] // TM
    tps_ctx = dims["seq"] // TM
    halo_per_tile = TM // SUBLANES
    n_halo = n_tok // SUBLANES

    def mod_row(i):
        return jnp.where(i < n_ctx_tiles, 0, 1 + jnp.maximum(i - n_ctx_tiles, 0) // tps_lat)

    tok = lambda w: pl.BlockSpec((TM, w), lambda i: (i, 0))
    full = lambda a: pl.BlockSpec(a.shape, lambda i: (0,) * a.ndim)
    sel_spec = pl.BlockSpec((PEER_HEADS, N_KEYS, TM), lambda i: (0, 0, i))
    return pl.pallas_call(
        functools.partial(_outproj_kernel, n_ctx_tiles=n_ctx_tiles, tps_ctx=tps_ctx, tps_lat=tps_lat),
        grid=(n_tiles,),
        in_specs=[
            tok(ATT_Q_W), tok(CM_W), tok(CV_W), tok(CV_W),
            pl.BlockSpec((SUBLANES, CV_W), lambda i: (jnp.maximum(i * halo_per_tile - 1, 0), 0)),
            pl.BlockSpec((SUBLANES, CV_W), lambda i: (jnp.minimum((i + 1) * halo_per_tile, n_halo - 1), 0)),
            tok(D_MODEL),
            pl.BlockSpec((1, N_MOD, D_MODEL), lambda i: (mod_row(i), 0, 0)),
            full(w_out), full(conv_w), full(g2), full(wp_hi), full(wp_lo), full(sub_keys),
        ],
        out_specs=[
            tok(D_MODEL), tok(D_MODEL), sel_spec, sel_spec,
            pl.BlockSpec((PEER_HEADS, 1, TM), lambda i: (0, 0, i)),
        ],
        out_shape=[
            jax.ShapeDtypeStruct((n_tok, D_MODEL), F32),
            jax.ShapeDtypeStruct((n_tok, D_MODEL), BF16),
            jax.ShapeDtypeStruct((PEER_HEADS, N_KEYS, n_tok), F32),
            jax.ShapeDtypeStruct((PEER_HEADS, N_KEYS, n_tok), F32),
            jax.ShapeDtypeStruct((PEER_HEADS, 1, n_tok), F32),
        ],
        scratch_shapes=[
            pltpu.VMEM((TM + 2 * SUBLANES, CV_W), F32),
            pltpu.VMEM((N_GROUPS, N_KEYS, TM), F32),
            pltpu.VMEM((N_GROUPS, TOPK, TM), F32),
        ],
        compiler_params=_params(("arbitrary",)),
        name="outproj",
    )(attn, cm, bg, zc, zc, zc, x, mod, w_out, conv_w, g2, wp_hi, wp_lo, sub_keys)


def _gelu_tanh(x):
    return x * (0.5 * (1.0 + jnp.tanh(math.sqrt(2.0 / math.pi) * (x + 0.044715 * (x * x * x)))))


def _peer_kernel(h2_ref, b1_ref, b2_ref, tau_ref, u_ref, vt_ref, o_ref, a_scr, w_scr):
    c = pl.program_id(1)
    tmp = h2_ref.shape[0]
    ec = u_ref.shape[0]
    keys_per_step = ec // N_KEYS
    lt_w = 2 * LANES
    a_scr[...] = _dot_nt(u_ref[...], h2_ref[...])

    def body(j, carry):
        i1 = c * keys_per_step + j
        r0 = pl.multiple_of(j * N_KEYS, N_KEYS)
        for lt in range(tmp // lt_w):
            lanes = slice(lt * lt_w, (lt + 1) * lt_w)
            gate = jnp.zeros((N_KEYS, lt_w), F32)
            for h in range(PEER_HEADS):
                t = b1_ref[h, pl.ds(i1, 1), lanes] + b2_ref[h, :, lanes]
                gate = gate + jnp.where(t >= tau_ref[h, :, lanes], jnp.exp(t), 0.0)
            a = a_scr[pl.ds(r0, N_KEYS), lanes]
            w_scr[pl.ds(r0, N_KEYS), lanes] = (gate * _gelu_tanh(a)).astype(BF16)
        return carry

    lax.fori_loop(0, keys_per_step, body, 0)
    contrib = _dot(vt_ref[...], w_scr[...])

    @pl.when(c == 0)
    def _():
        o_ref[...] = contrib

    @pl.when(c != 0)
    def _():
        o_ref[...] += contrib


def _peer(h2, b1, b2, tau, u_bf, vt_bf, dims):
    n_tok = dims["n_tok"]
    return pl.pallas_call(
        _peer_kernel,
        grid=(n_tok // TMP, N_EXPERTS // EC),
        in_specs=[
            pl.BlockSpec((TMP, D_MODEL), lambda t, c: (t, 0)),
            pl.BlockSpec((PEER_HEADS, N_KEYS, TMP), lambda t, c: (0, 0, t)),
            pl.BlockSpec((PEER_HEADS, N_KEYS, TMP), lambda t, c: (0, 0, t)),
            pl.BlockSpec((PEER_HEADS, 1, TMP), lambda t, c: (0, 0, t)),
            pl.BlockSpec((EC, D_MODEL), lambda t, c: (c, 0)),
            pl.BlockSpec((D_MODEL, EC), lambda t, c: (0, c)),
        ],
        out_specs=pl.BlockSpec((D_MODEL, TMP), lambda t, c: (0, t)),
        out_shape=jax.ShapeDtypeStruct((D_MODEL, n_tok), F32),
        scratch_shapes=[pltpu.VMEM((EC, TMP), F32), pltpu.VMEM((EC, TMP), BF16)],
        compiler_params=_params(("arbitrary", "arbitrary")),
        name="peer",
    )(h2, b1, b2, tau, u_bf, vt_bf)


def _final_kernel(x_ref, ft_ref, mod_ref, g_ref, o_ref):
    x = x_ref[...] + mod_ref[0, 5:6, :] * ft_ref[...].T
    o_ref[...] = _rms_rows(x) * g_ref[...]


def _final(x1, ffn_t, mod, g_final, row0, n_rows, mod_row):
    off = row0 // TM
    return pl.pallas_call(
        _final_kernel,
        grid=(n_rows // TM,),
        in_specs=[
            pl.BlockSpec((TM, D_MODEL), lambda i: (off + i, 0)),
            pl.BlockSpec((D_MODEL, TM), lambda i: (0, off + i)),
            pl.BlockSpec((1, N_MOD, D_MODEL), lambda i: (mod_row(i), 0, 0)),
            pl.BlockSpec((1, D_MODEL), lambda i: (0, 0)),
        ],
        out_specs=pl.BlockSpec((TM, D_MODEL), lambda i: (i, 0)),
        out_shape=jax.ShapeDtypeStruct((n_rows, D_MODEL), F32),
        compiler_params=_params(("arbitrary",)),
        name="final",
    )(x1, ffn_t, mod, g_final)


def _rope_tables(n_tokens):
    rows = n_tokens // GRID_W
    t = jnp.arange(rows * GRID_W)
    pos = jnp.stack([t // GRID_W, t % GRID_W], axis=-1).astype(F32)
    n_freq = HEAD_DIM // 4
    inv = ROPE_THETA ** (-jnp.arange(n_freq, dtype=F32) / n_freq)
    ang = pos[:, :, None] * inv
    ang = jnp.broadcast_to(ang[:, :, None, :], (ang.shape[0], 2, 2, n_freq)).reshape(-1, HEAD_DIM)
    cos, sin = jnp.cos(ang), jnp.sin(ang)
    first = (jnp.arange(HEAD_DIM) % (2 * n_freq)) < n_freq
    return cos, jnp.where(first, -sin, 0.0), jnp.where(first, 0.0, sin)


def kernel(x_prompt, x_sample, cache_k, cache_v, c, c_ctx, w_ada, b_ada, g_norm1, g_norm2, w_in, q_gain, k_gain,
           w_s, b_s, conv_w, w_out, w_pq, sub_keys, u_tab, v_tab, g_final):
    batch, seq, _ = x_prompt.shape
    dec_batch, dec_seq, _ = x_sample.shape
    depth = w_in.shape[0]
    past = cache_k.shape[2]
    n_ctx = batch * seq
    n_tok = n_ctx + dec_batch * dec_seq
    dims = dict(batch=batch, seq=seq, dec_batch=dec_batch, dec_seq=dec_seq, n_ctx=n_ctx, n_tok=n_tok)
    assert seq % TM == 0 and dec_seq % TM == 0 and n_ctx % dec_seq == 0 and n_tok % TMP == 0
    assert 1 + dec_batch <= COND_ROWS

    x = jnp.concatenate([x_prompt.reshape(n_ctx, D_MODEL), x_sample.reshape(dec_batch * dec_seq, D_MODEL)], axis=0)
    cond = jnp.zeros((COND_ROWS, D_MODEL), F32).at[0].set(c_ctx).at[1:1 + dec_batch].set(c)
    mod_all = _ada(cond, w_ada, b_ada).reshape(depth, COND_ROWS, N_MOD, D_MODEL)
    cos, sa, sb = _rope_tables(dec_seq)

    tps_lat = dec_seq // TM
    n_ctx_tiles = n_ctx // TM
    ffn_t = None
    mod_prev = None
    new_k, new_v = [], []
    for l in range(depth):
        mod = mod_all[l]
        w_in_bf = w_in[l].astype(BF16)
        w_out_bf = w_out[l].astype(BF16)
        wp_hi = w_pq[l].astype(BF16)
        wp_lo = (w_pq[l] - wp_hi.astype(F32)).astype(BF16)
        u_bf = u_tab[l].astype(BF16)
        vt_bf = v_tab[l].astype(BF16).T
        ws_bf = w_s[l].astype(BF16)
        bs_b = jnp.broadcast_to(b_s[l][:, :, None], (CM_HEADS, CHUNK, CM_DIM))
        ck = cache_k[:, l].reshape(dec_batch, past, ATT_KV_W).astype(BF16)
        cv = cache_v[:, l].reshape(dec_batch, past, ATT_KV_W).astype(BF16)

        outs = _inproj(x, ffn_t, mod_prev, mod, g_norm1[l][None], w_in_bf, q_gain[l][None], k_gain[l][None],
                       cos, sa, sb, ws_bf, bs_b, dims)
        if ffn_t is not None:
            x = outs[0]
            outs = outs[1:]
        q, kr, vb, kf, vf, cm, bg, zc = outs
        new_k.append(kf.reshape(batch, seq, N_KV, HEAD_DIM))
        new_v.append(vf.reshape(batch, seq, N_KV, HEAD_DIM))
        attn = _attention(q, kr, vb, ck, cv, dims)
        x, h2, b1, b2, tau = _outproj(attn, cm, bg, zc, x, mod, w_out_bf, conv_w[l], g_norm2[l][None], wp_hi, wp_lo,
                                      sub_keys[l].reshape(N_GROUPS, N_KEYS, PK_HALF), dims)
        ffn_t = _peer(h2, b1, b2, tau, u_bf, vt_bf, dims)
        mod_prev = mod

    gf = g_final[None]
    y_prompt = _final(x, ffn_t, mod_prev, gf, 0, n_ctx, lambda i: 0)
    y_sample = _final(x, ffn_t, mod_prev, gf, n_ctx, n_tok - n_ctx, lambda i: 1 + i // tps_lat)
    return (y_prompt.reshape(batch, seq, D_MODEL), y_sample.reshape(dec_batch, dec_seq, D_MODEL),
            jnp.stack(new_k, axis=1), jnp.stack(new_v, axis=1))
```

```python
import functools
import math

import jax
import jax.numpy as jnp
from jax import lax
from jax.experimental import pallas as pl
from jax.experimental.pallas import tpu as pltpu

F32 = jnp.float32
BF16 = jnp.bfloat16

D_MODEL = 2048
GRID_W = 64
N_HEADS = 8
N_KV = 2
HEAD_DIM = 128
GROUP = N_HEADS // N_KV
ATT_Q_W = N_HEADS * HEAD_DIM
ATT_KV_W = N_KV * HEAD_DIM
ROPE_THETA = 10000.0
CHUNK = 128
CM_HEADS = 4
CM_DIM = 128
CM_W = CM_HEADS * CM_DIM
CV_W = 512
D_IN = ATT_Q_W + 2 * ATT_KV_W + 2 * CM_W + 3 * CV_W
PEER_HEADS = 8
N_KEYS = 128
N_EXPERTS = N_KEYS * N_KEYS
TOPK = 16
PK_HALF = 128
N_GROUPS = 2 * PEER_HEADS
N_MOD = 6
EPS = 1e-6
NEG_INF = float("-inf")
LOG2E = math.log2(math.e)

LANES = 128
SUBLANES = 8
VMEM_LIMIT = 56 * 1024 * 1024

TM = 256
TQ = 128
TMP = 512
EC = 1024
PEER_SUB = 2
ADA_NT = 1024
COND_ROWS = 16

O_Q = 0
O_K = O_Q + ATT_Q_W
O_V = O_K + ATT_KV_W
O_UCM = O_V + ATT_KV_W
O_VCM = O_UCM + CM_W
O_BCV = O_VCM + CM_W
O_CCV = O_BCV + CV_W
O_HCV = O_CCV + CV_W


def _dot(a, b):
    return jnp.dot(a, b, preferred_element_type=F32)


def _dot_nt(a, b):
    return lax.dot_general(a, b, (((1,), (1,)), ((), ())), preferred_element_type=F32)


def _split(x):
    hi = x.astype(BF16)
    lo = (x - hi.astype(F32)).astype(BF16)
    return hi, lo


def _rms_rows(x):
    return x * lax.rsqrt(jnp.mean(x * x, axis=-1, keepdims=True) + EPS)


def _params(sem):
    return pltpu.CompilerParams(dimension_semantics=sem, vmem_limit_bytes=VMEM_LIMIT)


def _ada_kernel(c_ref, w_ref, b_ref, o_ref):
    c = c_ref[...]
    a = c * jax.nn.sigmoid(c)
    a_hi, a_lo = _split(a)
    w_hi, w_lo = _split(w_ref[0])
    o_ref[0] = _dot(a_hi, w_hi) + _dot(a_lo, w_hi) + _dot(a_hi, w_lo) + b_ref[0]


def _ada(cond, w_ada, b_ada):
    depth, _, n_out = w_ada.shape
    return pl.pallas_call(
        _ada_kernel,
        grid=(depth, n_out // ADA_NT),
        in_specs=[
            pl.BlockSpec((COND_ROWS, D_MODEL), lambda l, j: (0, 0)),
            pl.BlockSpec((1, D_MODEL, ADA_NT), lambda l, j: (l, 0, j)),
            pl.BlockSpec((1, 1, ADA_NT), lambda l, j: (l, 0, j)),
        ],
        out_specs=pl.BlockSpec((1, COND_ROWS, ADA_NT), lambda l, j: (l, 0, j)),
        out_shape=jax.ShapeDtypeStruct((depth, COND_ROWS, n_out), F32),
        compiler_params=_params(("arbitrary", "arbitrary")),
        name="ada",
    )(cond, w_ada, b_ada.reshape(depth, 1, n_out))


def _inproj_kernel(*refs, has_ffn, n_ctx_tiles):
    refs = list(refs)
    x_ref = refs.pop(0)
    if has_ffn:
        ft_ref = refs.pop(0)
        modp_ref = refs.pop(0)
    (mod_ref, g1_ref, w_ref, qg_ref, kg_ref, cos_ref, sa_ref, sb_ref, ws_ref, bs_ref) = refs[:10]
    outs = refs[10:]
    if has_ffn:
        xo_ref = outs.pop(0)
    q_ref, kr_ref, vb_ref, kf_ref, vf_ref, cm_ref, bg_ref, zc_ref = outs

    i = pl.program_id(0)
    is_lat = i >= n_ctx_tiles
    x = x_ref[...]
    if has_ffn:
        x = x + modp_ref[0, 5:6, :] * ft_ref[...].T
        xo_ref[...] = x
    sh1 = mod_ref[0, 0:1, :]
    sc1 = mod_ref[0, 1:2, :]
    h = _rms_rows(x) * g1_ref[...]
    hb = (h * (1.0 + sc1) + sh1).astype(BF16)

    cos = jnp.where(is_lat, cos_ref[...], 1.0)
    sa = jnp.where(is_lat, sa_ref[...], 0.0)
    sb = jnp.where(is_lat, sb_ref[...], 0.0)

    def rope(y):
        return y * cos + pltpu.roll(y, 96, 1) * sa + pltpu.roll(y, 32, 1) * sb

    zq = _dot(hb, w_ref[:, O_Q:O_K])
    for hd in range(N_HEADS):
        sl = slice(hd * HEAD_DIM, (hd + 1) * HEAD_DIM)
        y = _rms_rows(zq[:, sl]) * qg_ref[...]
        q_ref[:, sl] = rope(y).astype(BF16)

    zk = _dot(hb, w_ref[:, O_K:O_V])
    zv = _dot(hb, w_ref[:, O_V:O_UCM])
    vb_ref[...] = zv.astype(BF16)
    kn = []
    for hd in range(N_KV):
        sl = slice(hd * HEAD_DIM, (hd + 1) * HEAD_DIM)
        y = _rms_rows(zk[:, sl]) * kg_ref[...]
        kn.append(y)
        kr_ref[:, sl] = rope(y).astype(BF16)

    @pl.when(jnp.logical_not(is_lat))
    def _():
        for hd in range(N_KV):
            kf_ref[:, hd * HEAD_DIM:(hd + 1) * HEAD_DIM] = kn[hd]
        vf_ref[...] = zv

    zu = _dot(hb, w_ref[:, O_UCM:O_VCM])
    zvc = _dot(hb, w_ref[:, O_VCM:O_BCV])
    n_chunks = zu.shape[0] // CHUNK
    for c in range(CM_HEADS):
        sl = slice(c * CM_DIM, (c + 1) * CM_DIM)
        vh = _rms_rows(zvc[:, sl]).astype(BF16)
        for n in range(n_chunks):
            rows = slice(n * CHUNK, (n + 1) * CHUNK)
            mixed = _dot(ws_ref[c], vh[rows]) + bs_ref[c]
            cm_ref[rows, sl] = (zu[rows, sl] * mixed).astype(BF16)

    bg_ref[...] = _dot(hb, w_ref[:, O_BCV:O_CCV])
    zc_ref[...] = _dot(hb, w_ref[:, O_CCV:O_HCV]) * _dot(hb, w_ref[:, O_HCV:D_IN])


def _inproj(x, ffn_t, mod_prev, mod, g1, w_in, q_gain, k_gain, cos, sa, sb, w_s, b_s, dims):
    n_tok, n_ctx, tps_lat = dims["n_tok"], dims["n_ctx"], dims["dec_seq"] // TM
    n_ctx_tiles = n_ctx // TM
    has_ffn = ffn_t is not None

    def mod_row(i):
        return jnp.where(i < n_ctx_tiles, 0, 1 + jnp.maximum(i - n_ctx_tiles, 0) // tps_lat)

    def pos_blk(i):
        return jnp.maximum(i - n_ctx_tiles, 0) % tps_lat

    def ctx_blk(i):
        return jnp.minimum(i, n_ctx_tiles - 1)

    tok = lambda w: pl.BlockSpec((TM, w), lambda i: (i, 0))
    full = lambda a: pl.BlockSpec(a.shape, lambda i: (0,) * a.ndim)
    mod_spec = pl.BlockSpec((1, N_MOD, D_MODEL), lambda i: (mod_row(i), 0, 0))
    rope_spec = pl.BlockSpec((TM, HEAD_DIM), lambda i: (pos_blk(i), 0))

    args, in_specs = [x], [tok(D_MODEL)]
    if has_ffn:
        args += [ffn_t, mod_prev]
        in_specs += [pl.BlockSpec((D_MODEL, TM), lambda i: (0, i)), mod_spec]
    args += [mod, g1, w_in, q_gain, k_gain, cos, sa, sb, w_s, b_s]
    in_specs += [mod_spec, full(g1), full(w_in), full(q_gain), full(k_gain),
                 rope_spec, rope_spec, rope_spec, full(w_s), full(b_s)]

    out_shape, out_specs = [], []
    if has_ffn:
        out_shape.append(jax.ShapeDtypeStruct((n_tok, D_MODEL), F32))
        out_specs.append(tok(D_MODEL))
    ctx_spec = pl.BlockSpec((TM, ATT_KV_W), lambda i: (ctx_blk(i), 0))
    out_shape += [
        jax.ShapeDtypeStruct((n_tok, ATT_Q_W), BF16),
        jax.ShapeDtypeStruct((n_tok, ATT_KV_W), BF16),
        jax.ShapeDtypeStruct((n_tok, ATT_KV_W), BF16),
        jax.ShapeDtypeStruct((n_ctx, ATT_KV_W), F32),
        jax.ShapeDtypeStruct((n_ctx, ATT_KV_W), F32),
        jax.ShapeDtypeStruct((n_tok, CM_W), BF16),
        jax.ShapeDtypeStruct((n_tok, CV_W), F32),
        jax.ShapeDtypeStruct((n_tok, CV_W), F32),
    ]
    out_specs += [tok(ATT_Q_W), tok(ATT_KV_W), tok(ATT_KV_W), ctx_spec, ctx_spec,
                  tok(CM_W), tok(CV_W), tok(CV_W)]
    return pl.pallas_call(
        functools.partial(_inproj_kernel, has_ffn=has_ffn, n_ctx_tiles=n_ctx_tiles),
        grid=(n_tok // TM,),
        in_specs=in_specs,
        out_specs=out_specs,
        out_shape=out_shape,
        compiler_params=_params(("arbitrary",)),
        name="inproj",
    )(*args)


def _attend(q_ref, parts, o_ref):
    t = q_ref.shape[0]
    scale = HEAD_DIM ** -0.5
    for kv in range(N_KV):
        ksl = slice(kv * HEAD_DIM, (kv + 1) * HEAD_DIM)
        qs = jnp.concatenate(
            [q_ref[:, (kv * GROUP + g) * HEAD_DIM:(kv * GROUP + g + 1) * HEAD_DIM] for g in range(GROUP)], axis=0)
        scores = [_dot_nt(qs, k_ref[:, ksl]) * scale for k_ref, _ in parts]
        m = scores[0].max(axis=-1, keepdims=True)
        for s in scores[1:]:
            m = jnp.maximum(m, s.max(axis=-1, keepdims=True))
        den = None
        num = None
        for s, (_, v_ref) in zip(scores, parts):
            p = jnp.exp(s - m)
            ps = p.sum(axis=-1, keepdims=True)
            pv = _dot(p.astype(BF16), v_ref[:, ksl])
            den = ps if den is None else den + ps
            num = pv if num is None else num + pv
        o = num / den
        for g in range(GROUP):
            hd = kv * GROUP + g
            o_ref[:, hd * HEAD_DIM:(hd + 1) * HEAD_DIM] = o[g * t:(g + 1) * t].astype(BF16)


def _attn_ctx_kernel(q_ref, k_ref, v_ref, o_ref):
    _attend(q_ref, [(k_ref, v_ref)], o_ref)


def _attn_lat_kernel(q_ref, k_ref, v_ref, ck_ref, cv_ref, prev_ref, o_ref):
    del prev_ref
    _attend(q_ref, [(k_ref, v_ref), (ck_ref, cv_ref)], o_ref)


def _attention(q, kr, vb, ck, cv, dims):
    n_tok, n_ctx = dims["n_tok"], dims["n_ctx"]
    batch, seq, dec_batch, dec_seq = dims["batch"], dims["seq"], dims["dec_batch"], dims["dec_seq"]
    past = ck.shape[1]
    attn = pl.pallas_call(
        _attn_ctx_kernel,
        grid=(batch,),
        in_specs=[
            pl.BlockSpec((seq, ATT_Q_W), lambda b: (b, 0)),
            pl.BlockSpec((seq, ATT_KV_W), lambda b: (b, 0)),
            pl.BlockSpec((seq, ATT_KV_W), lambda b: (b, 0)),
        ],
        out_specs=pl.BlockSpec((seq, ATT_Q_W), lambda b: (b, 0)),
        out_shape=jax.ShapeDtypeStruct((n_tok, ATT_Q_W), BF16),
        compiler_params=_params(("arbitrary",)),
        name="attn_ctx",
    )(q, kr, vb)
    nq = dec_seq // TQ
    q_off = n_ctx // TQ
    kv_off = n_ctx // dec_seq
    return pl.pallas_call(
        _attn_lat_kernel,
        grid=(dec_batch, nq),
        in_specs=[
            pl.BlockSpec((TQ, ATT_Q_W), lambda b, j: (q_off + b * nq + j, 0)),
            pl.BlockSpec((dec_seq, ATT_KV_W), lambda b, j: (kv_off + b, 0)),
            pl.BlockSpec((dec_seq, ATT_KV_W), lambda b, j: (kv_off + b, 0)),
            pl.BlockSpec((None, past, ATT_KV_W), lambda b, j: (b, 0, 0)),
            pl.BlockSpec((None, past, ATT_KV_W), lambda b, j: (b, 0, 0)),
            pl.BlockSpec(memory_space=pl.ANY),
        ],
        out_specs=pl.BlockSpec((TQ, ATT_Q_W), lambda b, j: (q_off + b * nq + j, 0)),
        out_shape=jax.ShapeDtypeStruct((n_tok, ATT_Q_W), BF16),
        input_output_aliases={5: 0},
        compiler_params=_params(("arbitrary", "arbitrary")),
        name="attn_lat",
    )(q, kr, vb, ck, cv, attn)


def _extract_top(vals, idxs, n):
    big = jnp.int32(1 << 30)
    maxima = []
    for _ in range(n):
        m = functools.reduce(jnp.maximum, vals).max(axis=0, keepdims=True)
        cand = [jnp.where(v == m, ix, big) for v, ix in zip(vals, idxs)]
        mi = functools.reduce(jnp.minimum, cand).min(axis=0, keepdims=True)
        vals = [jnp.where(ix == mi, NEG_INF, v) for v, ix in zip(vals, idxs)]
        maxima.append(m)
    return maxima, vals


def _pair_blocks(m1, m2, r8):
    vals = [m1[0:1] + m2[0:8], m1[0:1] + m2[8:16], m1[1:2] + m2[0:8]]
    idxs = [r8, r8 + 8, r8 + TOPK]
    for i in range(2, 8):
        vals.append(jnp.where(r8 < TOPK // (i + 1), m1[i:i + 1] + m2[0:8], NEG_INF))
        idxs.append(r8 + TOPK * i)
    vals.append(m1[8:16] + m2[0:1])
    idxs.append((r8 + 8) * TOPK)
    return vals, idxs


def _outproj_kernel(attn_ref, cm_ref, bg_ref, zc_ref, zp_ref, zn_ref, x_ref, mod_ref, wo_ref, cw_ref, g2_ref,
                    wph_ref, wpl_ref, sk_ref,
                    x1_ref, h2_ref, b1_ref, b2_ref, tau_ref,
                    zpad_scr, st_scr, m_scr, *, n_ctx_tiles, tps_ctx, tps_lat):
    i = pl.program_id(0)
    tm = x_ref.shape[0]
    is_lat = i >= n_ctx_tiles
    jj = jnp.where(is_lat, jnp.maximum(i - n_ctx_tiles, 0) % tps_lat, i % tps_ctx)
    nt = jnp.where(is_lat, tps_lat, tps_ctx)

    zc = zc_ref[...]
    zpad_scr[0:SUBLANES, :] = jnp.where(jj == 0, 0.0, zp_ref[...])
    zpad_scr[SUBLANES:SUBLANES + tm, :] = zc
    zpad_scr[SUBLANES + tm:2 * SUBLANES + tm, :] = jnp.where(jj == nt - 1, 0.0, zn_ref[...])
    zprev = zpad_scr[SUBLANES - 1:SUBLANES - 1 + tm, :]
    znext = zpad_scr[SUBLANES + 1:SUBLANES + 1 + tm, :]
    y = zprev * cw_ref[0:1, :] + zc * cw_ref[1:2, :] + znext * cw_ref[2:3, :]
    cv = (bg_ref[...] * y).astype(BF16)

    mix = _dot(jnp.concatenate([attn_ref[...], cm_ref[...], cv], axis=1), wo_ref[...])
    g1 = mod_ref[0, 2:3, :]
    sh2 = mod_ref[0, 3:4, :]
    sc2 = mod_ref[0, 4:5, :]
    x1 = x_ref[...] + g1 * mix
    x1_ref[...] = x1
    h2 = (_rms_rows(x1) * g2_ref[...]) * (1.0 + sc2) + sh2
    h_hi, h_lo = _split(h2)
    h2_ref[...] = h_hi

    qp = _dot(h_hi, wph_ref[...]) + _dot(h_lo, wph_ref[...]) + _dot(h_hi, wpl_ref[...])
    for g in range(N_GROUPS):
        q_hi, q_lo = _split(qp[:, g * PK_HALF:(g + 1) * PK_HALF])
        k_hi, k_lo = _split(sk_ref[g])
        st_scr[g] = _dot_nt(k_hi, q_hi) + _dot_nt(k_hi, q_lo) + _dot_nt(k_lo, q_hi)

    n_lt = tm // LANES

    def stage1(g, carry):
        for lt in range(n_lt):
            lanes = slice(lt * LANES, (lt + 1) * LANES)
            s = st_scr[g, :, lanes]
            rows = lax.broadcasted_iota(jnp.int32, s.shape, 0)
            maxima, left = _extract_top([s], [rows], TOPK)
            m_scr[g, :, lanes] = jnp.concatenate(maxima, axis=0)
            st_scr[g, :, lanes] = jnp.where(left[0] == s, NEG_INF, s)
        return carry

    lax.fori_loop(0, N_GROUPS, stage1, 0)

    def stage2(h, carry):
        for lt in range(n_lt):
            lanes = slice(lt * LANES, (lt + 1) * LANES)
            m1 = m_scr[2 * h, :, lanes]
            m2 = m_scr[2 * h + 1, :, lanes]
            top = m1[0:1] + m2[0:1]
            r8 = lax.broadcasted_iota(jnp.int32, (SUBLANES, LANES), 0)
            vals, idxs = _pair_blocks(m1, m2, r8)
            _, left = _extract_top(vals, idxs, TOPK)
            picked = [l != v for l, v in zip(left, vals)]
            z = functools.reduce(
                jnp.add, [jnp.where(p, jnp.exp(v - top), 0.0) for p, v in zip(picked, vals)]).sum(axis=0, keepdims=True)
            lz = jnp.log(z)
            shifted, _ = _pair_blocks(((m1 - top) - lz) * LOG2E, m2 * LOG2E, r8)
            tau = functools.reduce(
                jnp.minimum, [jnp.where(p, v, jnp.inf) for p, v in zip(picked, shifted)]).min(axis=0, keepdims=True)
            tau_ref[h, :, lanes] = tau
            b1_ref[h, lt] = ((st_scr[2 * h, :, lanes] - top) - lz) * LOG2E
            b2_ref[h, :, lanes] = st_scr[2 * h + 1, :, lanes] * LOG2E
        return carry

    lax.fori_loop(0, PEER_HEADS, stage2, 0)


def _outproj(attn, cm, bg, zc, x, mod, w_out, conv_w, g2, wp_hi, wp_lo, sub_keys, dims):
    n_tok, n_ctx = dims["n_tok"], dims["n_ctx"]
    n_ctx_tiles = n_ctx // TM
    n_tiles = n_tok // TM
    tps_lat = dims["dec_seq"] // TM
    tps_ctx = dims["seq"] // TM
    halo_per_tile = TM // SUBLANES
    n_halo = n_tok // SUBLANES

    def mod_row(i):
        return jnp.where(i < n_ctx_tiles, 0, 1 + jnp.maximum(i - n_ctx_tiles, 0) // tps_lat)

    tok = lambda w: pl.BlockSpec((TM, w), lambda i: (i, 0))
    full = lambda a: pl.BlockSpec(a.shape, lambda i: (0,) * a.ndim)
    sel_spec = pl.BlockSpec((PEER_HEADS, N_KEYS, TM), lambda i: (0, 0, i))
    return pl.pallas_call(
        functools.partial(_outproj_kernel, n_ctx_tiles=n_ctx_tiles, tps_ctx=tps_ctx, tps_lat=tps_lat),
        grid=(n_tiles,),
        in_specs=[
            tok(ATT_Q_W), tok(CM_W), tok(CV_W), tok(CV_W),
            pl.BlockSpec((SUBLANES, CV_W), lambda i: (jnp.maximum(i * halo_per_tile - 1, 0), 0)),
            pl.BlockSpec((SUBLANES, CV_W), lambda i: (jnp.minimum((i + 1) * halo_per_tile, n_halo - 1), 0)),
            tok(D_MODEL),
            pl.BlockSpec((1, N_MOD, D_MODEL), lambda i: (mod_row(i), 0, 0)),
            full(w_out), full(conv_w), full(g2), full(wp_hi), full(wp_lo), full(sub_keys),
        ],
        out_specs=[
            tok(D_MODEL), tok(D_MODEL),
            pl.BlockSpec((PEER_HEADS, TM // LANES, N_KEYS, LANES), lambda i: (0, i, 0, 0)),
            sel_spec,
            pl.BlockSpec((PEER_HEADS, 1, TM), lambda i: (0, 0, i)),
        ],
        out_shape=[
            jax.ShapeDtypeStruct((n_tok, D_MODEL), F32),
            jax.ShapeDtypeStruct((n_tok, D_MODEL), BF16),
            jax.ShapeDtypeStruct((PEER_HEADS, n_tok // LANES, N_KEYS, LANES), F32),
            jax.ShapeDtypeStruct((PEER_HEADS, N_KEYS, n_tok), F32),
            jax.ShapeDtypeStruct((PEER_HEADS, 1, n_tok), F32),
        ],
        scratch_shapes=[
            pltpu.VMEM((TM + 2 * SUBLANES, CV_W), F32),
            pltpu.VMEM((N_GROUPS, N_KEYS, TM), F32),
            pltpu.VMEM((N_GROUPS, TOPK, TM), F32),
        ],
        compiler_params=_params(("arbitrary",)),
        name="outproj",
    )(attn, cm, bg, zc, zc, zc, x, mod, w_out, conv_w, g2, wp_hi, wp_lo, sub_keys)


def _gelu_tanh(x):
    return x * (0.5 * (1.0 + jnp.tanh(math.sqrt(2.0 / math.pi) * (x + 0.044715 * (x * x * x)))))


def _peer_kernel(h2_ref, b1_ref, b2_ref, tau_ref, u_ref, vt_ref, o_ref, *scratch):
    a_scrs, w_scrs = scratch[:PEER_SUB], scratch[PEER_SUB:]
    c = pl.program_id(1)
    tmp = h2_ref.shape[0]
    ec = u_ref.shape[0]
    keys_per_step = ec // N_KEYS

    @pl.when(c == 0)
    def _():
        o_ref[...] = jnp.zeros_like(o_ref)

    n_sub = len(a_scrs)
    es = ec // n_sub
    keys_per_sub = es // N_KEYS
    def scores(s):
        a_scrs[s][...] = _dot_nt(u_ref[s * es:(s + 1) * es, :], h2_ref[...])

    scores(0)
    for s in range(n_sub):
        if s + 1 < n_sub:
            scores(s + 1)
        for j in range(keys_per_sub):
            i1 = c * keys_per_step + s * keys_per_sub + j
            rows = slice(j * N_KEYS, (j + 1) * N_KEYS)
            for tb in range(tmp // LANES):
                tl = slice(tb * LANES, (tb + 1) * LANES)
                gate = None
                for h in range(PEER_HEADS):
                    t = b1_ref[h, tb, pl.ds(i1, N_KEYS, stride=0), :] + b2_ref[h, :, tl]
                    g = jnp.where(t >= tau_ref[h, :, tl], jnp.exp2(t), 0.0)
                    gate = g if gate is None else gate + g
                w_scrs[s][rows, tl] = (gate * _gelu_tanh(a_scrs[s][rows, tl])).astype(BF16)
        o_ref[...] += _dot(vt_ref[:, s * es:(s + 1) * es], w_scrs[s][...])


def _peer(h2, b1, b2, tau, u_bf, vt_bf, dims):
    n_tok = dims["n_tok"]
    return pl.pallas_call(
        _peer_kernel,
        grid=(n_tok // TMP, N_EXPERTS // EC),
        in_specs=[
            pl.BlockSpec((TMP, D_MODEL), lambda t, c: (t, 0)),
            pl.BlockSpec((PEER_HEADS, TMP // LANES, N_KEYS, LANES), lambda t, c: (0, t, 0, 0)),
            pl.BlockSpec((PEER_HEADS, N_KEYS, TMP), lambda t, c: (0, 0, t)),
            pl.BlockSpec((PEER_HEADS, 1, TMP), lambda t, c: (0, 0, t)),
            pl.BlockSpec((EC, D_MODEL), lambda t, c: (c, 0)),
            pl.BlockSpec((D_MODEL, EC), lambda t, c: (0, c)),
        ],
        out_specs=pl.BlockSpec((D_MODEL, TMP), lambda t, c: (0, t)),
        out_shape=jax.ShapeDtypeStruct((D_MODEL, n_tok), F32),
        scratch_shapes=([pltpu.VMEM((EC // PEER_SUB, TMP), F32)] * PEER_SUB
                        + [pltpu.VMEM((EC // PEER_SUB, TMP), BF16)] * PEER_SUB),
        compiler_params=_params(("arbitrary", "arbitrary")),
        name="peer",
    )(h2, b1, b2, tau, u_bf, vt_bf)


def _final_kernel(x_ref, ft_ref, mod_ref, g_ref, o_ref):
    x = x_ref[...] + mod_ref[0, 5:6, :] * ft_ref[...].T
    o_ref[...] = _rms_rows(x) * g_ref[...]


def _final(x1, ffn_t, mod, g_final, row0, n_rows, mod_row):
    off = row0 // TM
    return pl.pallas_call(
        _final_kernel,
        grid=(n_rows // TM,),
        in_specs=[
            pl.BlockSpec((TM, D_MODEL), lambda i: (off + i, 0)),
            pl.BlockSpec((D_MODEL, TM), lambda i: (0, off + i)),
            pl.BlockSpec((1, N_MOD, D_MODEL), lambda i: (mod_row(i), 0, 0)),
            pl.BlockSpec((1, D_MODEL), lambda i: (0, 0)),
        ],
        out_specs=pl.BlockSpec((TM, D_MODEL), lambda i: (i, 0)),
        out_shape=jax.ShapeDtypeStruct((n_rows, D_MODEL), F32),
        compiler_params=_params(("arbitrary",)),
        name="final",
    )(x1, ffn_t, mod, g_final)


def _rope_tables(n_tokens):
    rows = n_tokens // GRID_W
    t = jnp.arange(rows * GRID_W)
    pos = jnp.stack([t // GRID_W, t % GRID_W], axis=-1).astype(F32)
    n_freq = HEAD_DIM // 4
    inv = ROPE_THETA ** (-jnp.arange(n_freq, dtype=F32) / n_freq)
    ang = pos[:, :, None] * inv
    ang = jnp.broadcast_to(ang[:, :, None, :], (ang.shape[0], 2, 2, n_freq)).reshape(-1, HEAD_DIM)
    cos, sin = jnp.cos(ang), jnp.sin(ang)
    first = (jnp.arange(HEAD_DIM) % (2 * n_freq)) < n_freq
    return cos, jnp.where(first, -sin, 0.0), jnp.where(first, 0.0, sin)


def kernel(x_prompt, x_sample, cache_k, cache_v, c, c_ctx, w_ada, b_ada, g_norm1, g_norm2, w_in, q_gain, k_gain,
           w_s, b_s, conv_w, w_out, w_pq, sub_keys, u_tab, v_tab, g_final):
    batch, seq, _ = x_prompt.shape
    dec_batch, dec_seq, _ = x_sample.shape
    depth = w_in.shape[0]
    past = cache_k.shape[2]
    n_ctx = batch * seq
    n_tok = n_ctx + dec_batch * dec_seq
    dims = dict(batch=batch, seq=seq, dec_batch=dec_batch, dec_seq=dec_seq, n_ctx=n_ctx, n_tok=n_tok)
    assert seq % TM == 0 and dec_seq % TM == 0 and n_ctx % dec_seq == 0 and n_tok % TMP == 0
    assert 1 + dec_batch <= COND_ROWS

    x = jnp.concatenate([x_prompt.reshape(n_ctx, D_MODEL), x_sample.reshape(dec_batch * dec_seq, D_MODEL)], axis=0)
    cond = jnp.zeros((COND_ROWS, D_MODEL), F32).at[0].set(c_ctx).at[1:1 + dec_batch].set(c)
    mod_all = _ada(cond, w_ada, b_ada).reshape(depth, COND_ROWS, N_MOD, D_MODEL)
    cos, sa, sb = _rope_tables(dec_seq)

    tps_lat = dec_seq // TM
    n_ctx_tiles = n_ctx // TM
    ffn_t = None
    mod_prev = None
    new_k, new_v = [], []
    for l in range(depth):
        mod = mod_all[l]
        w_in_bf = w_in[l].astype(BF16)
        w_out_bf = w_out[l].astype(BF16)
        wp_hi = w_pq[l].astype(BF16)
        wp_lo = (w_pq[l] - wp_hi.astype(F32)).astype(BF16)
        u_bf = u_tab[l].astype(BF16)
        vt_bf = v_tab[l].astype(BF16).T
        ws_bf = w_s[l].astype(BF16)
        bs_b = jnp.broadcast_to(b_s[l][:, :, None], (CM_HEADS, CHUNK, CM_DIM))
        ck = cache_k[:, l].reshape(dec_batch, past, ATT_KV_W).astype(BF16)
        cv = cache_v[:, l].reshape(dec_batch, past, ATT_KV_W).astype(BF16)

        outs = _inproj(x, ffn_t, mod_prev, mod, g_norm1[l][None], w_in_bf, q_gain[l][None], k_gain[l][None],
                       cos, sa, sb, ws_bf, bs_b, dims)
        if ffn_t is not None:
            x = outs[0]
            outs = outs[1:]
        q, kr, vb, kf, vf, cm, bg, zc = outs
        new_k.append(kf.reshape(batch, seq, N_KV, HEAD_DIM))
        new_v.append(vf.reshape(batch, seq, N_KV, HEAD_DIM))
        attn = _attention(q, kr, vb, ck, cv, dims)
        x, h2, b1, b2, tau = _outproj(attn, cm, bg, zc, x, mod, w_out_bf, conv_w[l], g_norm2[l][None], wp_hi, wp_lo,
                                      sub_keys[l].reshape(N_GROUPS, N_KEYS, PK_HALF), dims)
        ffn_t = _peer(h2, b1, b2, tau, u_bf, vt_bf, dims)
        mod_prev = mod

    gf = g_final[None]
    y_prompt = _final(x, ffn_t, mod_prev, gf, 0, n_ctx, lambda i: 0)
    y_sample = _final(x, ffn_t, mod_prev, gf, n_ctx, n_tok - n_ctx, lambda i: 1 + i // tps_lat)
    return (y_prompt.reshape(batch, seq, D_MODEL), y_sample.reshape(dec_batch, dec_seq, D_MODEL),
            jnp.stack(new_k, axis=1), jnp.stack(new_v, axis=1))
```

```python
import functools
import math

import jax
import jax.numpy as jnp
from jax import lax
from jax.experimental import pallas as pl
from jax.experimental.pallas import tpu as pltpu

F32 = jnp.float32
BF16 = jnp.bfloat16

D_MODEL = 2048
GRID_W = 64
N_HEADS = 8
N_KV = 2
HEAD_DIM = 128
GROUP = N_HEADS // N_KV
ATT_Q_W = N_HEADS * HEAD_DIM
ATT_KV_W = N_KV * HEAD_DIM
ROPE_THETA = 10000.0
CHUNK = 128
CM_HEADS = 4
CM_DIM = 128
CM_W = CM_HEADS * CM_DIM
CV_W = 512
D_IN = ATT_Q_W + 2 * ATT_KV_W + 2 * CM_W + 3 * CV_W
PEER_HEADS = 8
N_KEYS = 128
N_EXPERTS = N_KEYS * N_KEYS
TOPK = 16
PK_HALF = 128
N_GROUPS = 2 * PEER_HEADS
N_MOD = 6
EPS = 1e-6
NEG_INF = float("-inf")
LOG2E = math.log2(math.e)

LANES = 128
SUBLANES = 8
VMEM_LIMIT = 56 * 1024 * 1024

TM = 256
TQ = 128
TMP = 1024
EC = 1024
PEER_SUB = 2
ADA_NT = 1024
COND_ROWS = 16

O_Q = 0
O_K = O_Q + ATT_Q_W
O_V = O_K + ATT_KV_W
O_UCM = O_V + ATT_KV_W
O_VCM = O_UCM + CM_W
O_BCV = O_VCM + CM_W
O_CCV = O_BCV + CV_W
O_HCV = O_CCV + CV_W


def _dot(a, b):
    return jnp.dot(a, b, preferred_element_type=F32)


def _dot_nt(a, b):
    return lax.dot_general(a, b, (((1,), (1,)), ((), ())), preferred_element_type=F32)


def _split(x):
    hi = x.astype(BF16)
    lo = (x - hi.astype(F32)).astype(BF16)
    return hi, lo


def _rms_rows(x):
    return x * lax.rsqrt(jnp.mean(x * x, axis=-1, keepdims=True) + EPS)


def _params(sem):
    return pltpu.CompilerParams(dimension_semantics=sem, vmem_limit_bytes=VMEM_LIMIT)


def _ada_kernel(c_ref, w_ref, b_ref, o_ref):
    c = c_ref[...]
    a = c * jax.nn.sigmoid(c)
    a_hi, a_lo = _split(a)
    w_hi, w_lo = _split(w_ref[0])
    o_ref[0] = _dot(a_hi, w_hi) + _dot(a_lo, w_hi) + _dot(a_hi, w_lo) + b_ref[0]


def _ada(cond, w_ada, b_ada):
    depth, _, n_out = w_ada.shape
    return pl.pallas_call(
        _ada_kernel,
        grid=(depth, n_out // ADA_NT),
        in_specs=[
            pl.BlockSpec((COND_ROWS, D_MODEL), lambda l, j: (0, 0)),
            pl.BlockSpec((1, D_MODEL, ADA_NT), lambda l, j: (l, 0, j)),
            pl.BlockSpec((1, 1, ADA_NT), lambda l, j: (l, 0, j)),
        ],
        out_specs=pl.BlockSpec((1, COND_ROWS, ADA_NT), lambda l, j: (l, 0, j)),
        out_shape=jax.ShapeDtypeStruct((depth, COND_ROWS, n_out), F32),
        compiler_params=_params(("arbitrary", "arbitrary")),
        name="ada",
    )(cond, w_ada, b_ada.reshape(depth, 1, n_out))


def _inproj_kernel(*refs, has_ffn, n_ctx_tiles):
    refs = list(refs)
    x_ref = refs.pop(0)
    if has_ffn:
        ft_ref = refs.pop(0)
        modp_ref = refs.pop(0)
    (mod_ref, g1_ref, w_ref, qg_ref, kg_ref, cos_ref, sa_ref, sb_ref, ws_ref, bs_ref) = refs[:10]
    outs = refs[10:]
    if has_ffn:
        xo_ref = outs.pop(0)
    q_ref, kr_ref, vb_ref, kf_ref, vf_ref, cm_ref, bg_ref, zc_ref = outs

    i = pl.program_id(0)
    is_lat = i >= n_ctx_tiles
    x = x_ref[...]
    if has_ffn:
        x = x + modp_ref[0, 5:6, :] * ft_ref[...].T
        xo_ref[...] = x
    sh1 = mod_ref[0, 0:1, :]
    sc1 = mod_ref[0, 1:2, :]
    h = _rms_rows(x) * g1_ref[...]
    hb = (h * (1.0 + sc1) + sh1).astype(BF16)

    cos = jnp.where(is_lat, cos_ref[...], 1.0)
    sa = jnp.where(is_lat, sa_ref[...], 0.0)
    sb = jnp.where(is_lat, sb_ref[...], 0.0)

    def rope(y):
        return y * cos + pltpu.roll(y, 96, 1) * sa + pltpu.roll(y, 32, 1) * sb

    zq = _dot(hb, w_ref[:, O_Q:O_K])
    for hd in range(N_HEADS):
        sl = slice(hd * HEAD_DIM, (hd + 1) * HEAD_DIM)
        y = _rms_rows(zq[:, sl]) * qg_ref[...]
        q_ref[:, sl] = rope(y).astype(BF16)

    zk = _dot(hb, w_ref[:, O_K:O_V])
    zv = _dot(hb, w_ref[:, O_V:O_UCM])
    vb_ref[...] = zv.astype(BF16)
    kn = []
    for hd in range(N_KV):
        sl = slice(hd * HEAD_DIM, (hd + 1) * HEAD_DIM)
        y = _rms_rows(zk[:, sl]) * kg_ref[...]
        kn.append(y)
        kr_ref[:, sl] = rope(y).astype(BF16)

    @pl.when(jnp.logical_not(is_lat))
    def _():
        for hd in range(N_KV):
            kf_ref[:, hd * HEAD_DIM:(hd + 1) * HEAD_DIM] = kn[hd]
        vf_ref[...] = zv

    zu = _dot(hb, w_ref[:, O_UCM:O_VCM])
    zvc = _dot(hb, w_ref[:, O_VCM:O_BCV])
    n_chunks = zu.shape[0] // CHUNK
    for c in range(CM_HEADS):
        sl = slice(c * CM_DIM, (c + 1) * CM_DIM)
        vh = _rms_rows(zvc[:, sl]).astype(BF16)
        for n in range(n_chunks):
            rows = slice(n * CHUNK, (n + 1) * CHUNK)
            mixed = _dot(ws_ref[c], vh[rows]) + bs_ref[c]
            cm_ref[rows, sl] = (zu[rows, sl] * mixed).astype(BF16)

    bg_ref[...] = _dot(hb, w_ref[:, O_BCV:O_CCV])
    zc_ref[...] = _dot(hb, w_ref[:, O_CCV:O_HCV]) * _dot(hb, w_ref[:, O_HCV:D_IN])


def _inproj(x, ffn_t, mod_prev, mod, g1, w_in, q_gain, k_gain, cos, sa, sb, w_s, b_s, dims):
    n_tok, n_ctx, tps_lat = dims["n_tok"], dims["n_ctx"], dims["dec_seq"] // TM
    n_ctx_tiles = n_ctx // TM
    has_ffn = ffn_t is not None

    def mod_row(i):
        return jnp.where(i < n_ctx_tiles, 0, 1 + jnp.maximum(i - n_ctx_tiles, 0) // tps_lat)

    def pos_blk(i):
        return jnp.maximum(i - n_ctx_tiles, 0) % tps_lat

    def ctx_blk(i):
        return jnp.minimum(i, n_ctx_tiles - 1)

    tok = lambda w: pl.BlockSpec((TM, w), lambda i: (i, 0))
    full = lambda a: pl.BlockSpec(a.shape, lambda i: (0,) * a.ndim)
    mod_spec = pl.BlockSpec((1, N_MOD, D_MODEL), lambda i: (mod_row(i), 0, 0))
    rope_spec = pl.BlockSpec((TM, HEAD_DIM), lambda i: (pos_blk(i), 0))

    args, in_specs = [x], [tok(D_MODEL)]
    if has_ffn:
        args += [ffn_t, mod_prev]
        in_specs += [pl.BlockSpec((D_MODEL, TM), lambda i: (0, i)), mod_spec]
    args += [mod, g1, w_in, q_gain, k_gain, cos, sa, sb, w_s, b_s]
    in_specs += [mod_spec, full(g1), full(w_in), full(q_gain), full(k_gain),
                 rope_spec, rope_spec, rope_spec, full(w_s), full(b_s)]

    out_shape, out_specs = [], []
    if has_ffn:
        out_shape.append(jax.ShapeDtypeStruct((n_tok, D_MODEL), F32))
        out_specs.append(tok(D_MODEL))
    ctx_spec = pl.BlockSpec((TM, ATT_KV_W), lambda i: (ctx_blk(i), 0))
    out_shape += [
        jax.ShapeDtypeStruct((n_tok, ATT_Q_W), BF16),
        jax.ShapeDtypeStruct((n_tok, ATT_KV_W), BF16),
        jax.ShapeDtypeStruct((n_tok, ATT_KV_W), BF16),
        jax.ShapeDtypeStruct((n_ctx, ATT_KV_W), F32),
        jax.ShapeDtypeStruct((n_ctx, ATT_KV_W), F32),
        jax.ShapeDtypeStruct((n_tok, CM_W), BF16),
        jax.ShapeDtypeStruct((n_tok, CV_W), F32),
        jax.ShapeDtypeStruct((n_tok, CV_W), F32),
    ]
    out_specs += [tok(ATT_Q_W), tok(ATT_KV_W), tok(ATT_KV_W), ctx_spec, ctx_spec,
                  tok(CM_W), tok(CV_W), tok(CV_W)]
    return pl.pallas_call(
        functools.partial(_inproj_kernel, has_ffn=has_ffn, n_ctx_tiles=n_ctx_tiles),
        grid=(n_tok // TM,),
        in_specs=in_specs,
        out_specs=out_specs,
        out_shape=out_shape,
        compiler_params=_params(("arbitrary",)),
        name="inproj",
    )(*args)


def _attend(q_ref, parts, o_ref):
    t = q_ref.shape[0]
    scale = HEAD_DIM ** -0.5
    for kv in range(N_KV):
        ksl = slice(kv * HEAD_DIM, (kv + 1) * HEAD_DIM)
        qs = jnp.concatenate(
            [q_ref[:, (kv * GROUP + g) * HEAD_DIM:(kv * GROUP + g + 1) * HEAD_DIM] for g in range(GROUP)], axis=0)
        scores = [_dot_nt(qs, k_ref[:, ksl]) * scale for k_ref, _ in parts]
        m = scores[0].max(axis=-1, keepdims=True)
        for s in scores[1:]:
            m = jnp.maximum(m, s.max(axis=-1, keepdims=True))
        den = None
        num = None
        for s, (_, v_ref) in zip(scores, parts):
            p = jnp.exp(s - m)
            ps = p.sum(axis=-1, keepdims=True)
            pv = _dot(p.astype(BF16), v_ref[:, ksl])
            den = ps if den is None else den + ps
            num = pv if num is None else num + pv
        o = num / den
        for g in range(GROUP):
            hd = kv * GROUP + g
            o_ref[:, hd * HEAD_DIM:(hd + 1) * HEAD_DIM] = o[g * t:(g + 1) * t].astype(BF16)


def _attn_ctx_kernel(q_ref, k_ref, v_ref, o_ref):
    _attend(q_ref, [(k_ref, v_ref)], o_ref)


def _attn_lat_kernel(q_ref, k_ref, v_ref, ck_ref, cv_ref, o_ref):
    _attend(q_ref, [(k_ref, v_ref), (ck_ref, cv_ref)], o_ref)


def _attention(q, kr, vb, ck, cv, dims):
    n_tok, n_ctx = dims["n_tok"], dims["n_ctx"]
    batch, seq, dec_batch, dec_seq = dims["batch"], dims["seq"], dims["dec_batch"], dims["dec_seq"]
    past = ck.shape[1]
    attn_ctx = pl.pallas_call(
        _attn_ctx_kernel,
        grid=(batch,),
        in_specs=[
            pl.BlockSpec((seq, ATT_Q_W), lambda b: (b, 0)),
            pl.BlockSpec((seq, ATT_KV_W), lambda b: (b, 0)),
            pl.BlockSpec((seq, ATT_KV_W), lambda b: (b, 0)),
        ],
        out_specs=pl.BlockSpec((seq, ATT_Q_W), lambda b: (b, 0)),
        out_shape=jax.ShapeDtypeStruct((n_ctx, ATT_Q_W), BF16),
        compiler_params=_params(("arbitrary",)),
        name="attn_ctx",
    )(q, kr, vb)
    nq = dec_seq // TQ
    q_off = n_ctx // TQ
    kv_off = n_ctx // dec_seq
    attn_lat = pl.pallas_call(
        _attn_lat_kernel,
        grid=(dec_batch, nq),
        in_specs=[
            pl.BlockSpec((TQ, ATT_Q_W), lambda b, j: (q_off + b * nq + j, 0)),
            pl.BlockSpec((dec_seq, ATT_KV_W), lambda b, j: (kv_off + b, 0)),
            pl.BlockSpec((dec_seq, ATT_KV_W), lambda b, j: (kv_off + b, 0)),
            pl.BlockSpec((None, past, ATT_KV_W), lambda b, j: (b, 0, 0)),
            pl.BlockSpec((None, past, ATT_KV_W), lambda b, j: (b, 0, 0)),
        ],
        out_specs=pl.BlockSpec((TQ, ATT_Q_W), lambda b, j: (b * nq + j, 0)),
        out_shape=jax.ShapeDtypeStruct((n_tok - n_ctx, ATT_Q_W), BF16),
        compiler_params=_params(("arbitrary", "arbitrary")),
        name="attn_lat",
    )(q, kr, vb, ck, cv)
    return attn_ctx, attn_lat


def _extract_top(vals, idxs, n):
    big = jnp.int32(1 << 30)
    maxima = []
    for _ in range(n):
        m = functools.reduce(jnp.maximum, vals).max(axis=0, keepdims=True)
        cand = [jnp.where(v == m, ix, big) for v, ix in zip(vals, idxs)]
        mi = functools.reduce(jnp.minimum, cand).min(axis=0, keepdims=True)
        vals = [jnp.where(ix == mi, NEG_INF, v) for v, ix in zip(vals, idxs)]
        maxima.append(m)
    return maxima, vals


def _oddeven_merge_sort_pairs(lo, hi):
    def merge(lo, hi, r):
        step = r * 2
        if step < hi - lo:
            yield from merge(lo, hi, step)
            yield from merge(lo + r, hi, step)
            for i in range(lo + r, hi - r, step):
                yield (i, i + r)
        else:
            yield (lo, lo + r)

    if hi - lo >= 1:
        mid = lo + (hi - lo) // 2
        yield from _oddeven_merge_sort_pairs(lo, mid)
        yield from _oddeven_merge_sort_pairs(mid + 1, hi)
        yield from merge(lo, hi, 1)


_SORT16 = tuple(_oddeven_merge_sort_pairs(0, TOPK - 1))


def _compare_exchange(v, i, j):
    if v[j] is None:
        return
    if v[i] is None:
        v[i], v[j] = v[j], None
        return
    v[i], v[j] = jnp.maximum(v[i], v[j]), jnp.minimum(v[i], v[j])


def _merge_sublanes(v):
    n = len(v)
    for shift in (4, 2, 1):
        y = [None if v[n - 1 - a] is None else pltpu.roll(v[n - 1 - a], shift, 0) for a in range(n)]
        v = [y[a] if v[a] is None else (v[a] if y[a] is None else jnp.maximum(v[a], y[a])) for a in range(n)]
        for d in (8, 4, 2, 1):
            for i in range(n):
                if not i & d:
                    _compare_exchange(v, i, i + d)
    return v


def _top16_sorted(s):
    blocks = [s[SUBLANES * a:SUBLANES * (a + 1), :] for a in range(N_KEYS // SUBLANES)]
    v = list(blocks)
    for i, j in _SORT16:
        _compare_exchange(v, i, j)
    v = _merge_sublanes(v)
    count = functools.reduce(jnp.add, [jnp.where(b >= v[TOPK - 1], 1.0, 0.0) for b in blocks]).sum(axis=0, keepdims=True)
    return v, count


def _stack_rows(v, r8):
    halves = []
    for base in (0, SUBLANES):
        acc = v[base]
        for k in range(1, SUBLANES):
            acc = jnp.where(r8 == k, v[base + k], acc)
        halves.append(acc)
    return jnp.concatenate(halves, axis=0)


def _kth_largest(vals):
    chain = [vals[0]] + list(vals[2:9])
    for extra in (vals[1], vals[9]):
        x = extra
        for k in range(len(chain)):
            chain[k], x = jnp.maximum(chain[k], x), jnp.minimum(chain[k], x)
        chain.append(x)
    v = _merge_sublanes(chain + [None] * (TOPK - len(chain)))
    return v[TOPK - 1][0:1]


def _pair_blocks(m1, m2, r8):
    vals = [m1[0:1] + m2[0:8], m1[0:1] + m2[8:16], m1[1:2] + m2[0:8]]
    idxs = [r8, r8 + 8, r8 + TOPK]
    for i in range(2, 8):
        vals.append(jnp.where(r8 < TOPK // (i + 1), m1[i:i + 1] + m2[0:8], NEG_INF))
        idxs.append(r8 + TOPK * i)
    vals.append(m1[8:16] + m2[0:1])
    idxs.append((r8 + 8) * TOPK)
    return vals, idxs


def _outproj_kernel(attn_c_ref, attn_l_ref, cm_ref, bg_ref, zc_ref, zp_ref, zn_ref, x_ref, mod_ref, wo_ref, cw_ref, g2_ref,
                    wph_ref, wpl_ref, sk_ref,
                    x1_ref, h2_ref, b1_ref, b2_ref, tau_ref,
                    zpad_scr, st_scr, m_scr, *, n_ctx_tiles, tps_ctx, tps_lat):
    i = pl.program_id(0)
    tm = x_ref.shape[0]
    is_lat = i >= n_ctx_tiles
    jj = jnp.where(is_lat, jnp.maximum(i - n_ctx_tiles, 0) % tps_lat, i % tps_ctx)
    nt = jnp.where(is_lat, tps_lat, tps_ctx)

    zc = zc_ref[...]
    zpad_scr[0:SUBLANES, :] = jnp.where(jj == 0, 0.0, zp_ref[...])
    zpad_scr[SUBLANES:SUBLANES + tm, :] = zc
    zpad_scr[SUBLANES + tm:2 * SUBLANES + tm, :] = jnp.where(jj == nt - 1, 0.0, zn_ref[...])
    zprev = zpad_scr[SUBLANES - 1:SUBLANES - 1 + tm, :]
    znext = zpad_scr[SUBLANES + 1:SUBLANES + 1 + tm, :]
    y = zprev * cw_ref[0:1, :] + zc * cw_ref[1:2, :] + znext * cw_ref[2:3, :]
    cv = (bg_ref[...] * y).astype(BF16)

    attn = jnp.where(is_lat, attn_l_ref[...], attn_c_ref[...])
    mix = _dot(jnp.concatenate([attn, cm_ref[...], cv], axis=1), wo_ref[...])
    g1 = mod_ref[0, 2:3, :]
    sh2 = mod_ref[0, 3:4, :]
    sc2 = mod_ref[0, 4:5, :]
    x1 = x_ref[...] + g1 * mix
    x1_ref[...] = x1
    h2 = (_rms_rows(x1) * g2_ref[...]) * (1.0 + sc2) + sh2
    h_hi, h_lo = _split(h2)
    h2_ref[...] = h_hi

    qp = _dot(h_hi, wph_ref[...]) + _dot(h_lo, wph_ref[...]) + _dot(h_hi, wpl_ref[...])
    for g in range(N_GROUPS):
        q_hi, q_lo = _split(qp[:, g * PK_HALF:(g + 1) * PK_HALF])
        k_hi, k_lo = _split(sk_ref[g])
        st_scr[g] = _dot_nt(k_hi, q_hi) + _dot_nt(k_hi, q_lo) + _dot_nt(k_lo, q_hi)

    n_lt = tm // LANES
    r8 = lax.broadcasted_iota(jnp.int32, (SUBLANES, LANES), 0)

    def finish(h, lt, lanes, a1, a2, top, picked, vals, m1, m2):
        z = functools.reduce(
            jnp.add, [jnp.where(p, jnp.exp(v - top), 0.0) for p, v in zip(picked, vals)]).sum(axis=0, keepdims=True)
        lz = jnp.log(z)
        shifted = _pair_blocks(((m1 - top) - lz) * LOG2E - 1.0, m2 * LOG2E, r8)[0]
        tau = functools.reduce(
            jnp.minimum, [jnp.where(p, v, jnp.inf) for p, v in zip(picked, shifted)]).min(axis=0, keepdims=True)
        tau_ref[h, :, lanes] = tau
        b1_ref[h, lt] = ((a1 - top) - lz) * LOG2E - 1.0
        b2_ref[h, :, lanes] = a2 * LOG2E

    def fast(h, bad):
        for lt in range(n_lt):
            lanes = slice(lt * LANES, (lt + 1) * LANES)
            s1 = st_scr[2 * h, :, lanes]
            s2 = st_scr[2 * h + 1, :, lanes]
            v1, n1 = _top16_sorted(s1)
            v2, n2 = _top16_sorted(s2)
            m1 = _stack_rows(v1, r8)
            m2 = _stack_rows(v2, r8)
            top = m1[0:1] + m2[0:1]
            vals = _pair_blocks(m1, m2, r8)[0]
            thr = _kth_largest(vals)
            picked = [v >= thr for v in vals]
            n12 = functools.reduce(jnp.add, [jnp.where(p, 1.0, 0.0) for p in picked]).sum(axis=0, keepdims=True)
            bad = jnp.maximum(bad, jnp.where((n1 == TOPK) & (n2 == TOPK) & (n12 == TOPK), 0.0, 1.0))
            a1 = jnp.where(s1 >= v1[TOPK - 1][0:1], s1, NEG_INF)
            a2 = jnp.where(s2 >= v2[TOPK - 1][0:1], s2, NEG_INF)
            finish(h, lt, lanes, a1, a2, top, picked, vals, m1, m2)
        return bad

    bad = lax.fori_loop(0, PEER_HEADS, fast, jnp.zeros((1, LANES), F32))

    @pl.when(jnp.max(bad) > 0.0)
    def _():
        def stage1(g, carry):
            for lt in range(n_lt):
                lanes = slice(lt * LANES, (lt + 1) * LANES)
                s = st_scr[g, :, lanes]
                rows = lax.broadcasted_iota(jnp.int32, s.shape, 0)
                maxima, left = _extract_top([s], [rows], TOPK)
                m_scr[g, :, lanes] = jnp.concatenate(maxima, axis=0)
                st_scr[g, :, lanes] = jnp.where(left[0] == s, NEG_INF, s)
            return carry

        lax.fori_loop(0, N_GROUPS, stage1, 0)

        def stage2(h, carry):
            for lt in range(n_lt):
                lanes = slice(lt * LANES, (lt + 1) * LANES)
                m1 = m_scr[2 * h, :, lanes]
                m2 = m_scr[2 * h + 1, :, lanes]
                top = m1[0:1] + m2[0:1]
                vals, idxs = _pair_blocks(m1, m2, r8)
                _, left = _extract_top(vals, idxs, TOPK)
                picked = [l != v for l, v in zip(left, vals)]
                finish(h, lt, lanes, st_scr[2 * h, :, lanes], st_scr[2 * h + 1, :, lanes], top, picked, vals, m1, m2)
            return carry

        lax.fori_loop(0, PEER_HEADS, stage2, 0)


def _outproj(attn_ctx, attn_lat, cm, bg, zc, x, mod, w_out, conv_w, g2, wp_hi, wp_lo, sub_keys, dims):
    n_tok, n_ctx = dims["n_tok"], dims["n_ctx"]
    n_ctx_tiles = n_ctx // TM
    n_tiles = n_tok // TM
    tps_lat = dims["dec_seq"] // TM
    tps_ctx = dims["seq"] // TM
    halo_per_tile = TM // SUBLANES
    n_halo = n_tok // SUBLANES

    def mod_row(i):
        return jnp.where(i < n_ctx_tiles, 0, 1 + jnp.maximum(i - n_ctx_tiles, 0) // tps_lat)

    tok = lambda w: pl.BlockSpec((TM, w), lambda i: (i, 0))
    full = lambda a: pl.BlockSpec(a.shape, lambda i: (0,) * a.ndim)
    sel_spec = pl.BlockSpec((PEER_HEADS, N_KEYS, TM), lambda i: (0, 0, i))
    return pl.pallas_call(
        functools.partial(_outproj_kernel, n_ctx_tiles=n_ctx_tiles, tps_ctx=tps_ctx, tps_lat=tps_lat),
        grid=(n_tiles,),
        in_specs=[
            pl.BlockSpec((TM, ATT_Q_W), lambda i: (jnp.minimum(i, n_ctx_tiles - 1), 0)),
            pl.BlockSpec((TM, ATT_Q_W), lambda i: (jnp.maximum(i - n_ctx_tiles, 0), 0)),
            tok(CM_W), tok(CV_W), tok(CV_W),
            pl.BlockSpec((SUBLANES, CV_W), lambda i: (jnp.maximum(i * halo_per_tile - 1, 0), 0)),
            pl.BlockSpec((SUBLANES, CV_W), lambda i: (jnp.minimum((i + 1) * halo_per_tile, n_halo - 1), 0)),
            tok(D_MODEL),
            pl.BlockSpec((1, N_MOD, D_MODEL), lambda i: (mod_row(i), 0, 0)),
            full(w_out), full(conv_w), full(g2), full(wp_hi), full(wp_lo), full(sub_keys),
        ],
        out_specs=[
            tok(D_MODEL), tok(D_MODEL),
            pl.BlockSpec((PEER_HEADS, TM // LANES, N_KEYS, LANES), lambda i: (0, i, 0, 0)),
            sel_spec,
            pl.BlockSpec((PEER_HEADS, 1, TM), lambda i: (0, 0, i)),
        ],
        out_shape=[
            jax.ShapeDtypeStruct((n_tok, D_MODEL), F32),
            jax.ShapeDtypeStruct((n_tok, D_MODEL), BF16),
            jax.ShapeDtypeStruct((PEER_HEADS, n_tok // LANES, N_KEYS, LANES), F32),
            jax.ShapeDtypeStruct((PEER_HEADS, N_KEYS, n_tok), F32),
            jax.ShapeDtypeStruct((PEER_HEADS, 1, n_tok), F32),
        ],
        scratch_shapes=[
            pltpu.VMEM((TM + 2 * SUBLANES, CV_W), F32),
            pltpu.VMEM((N_GROUPS, N_KEYS, TM), F32),
            pltpu.VMEM((N_GROUPS, TOPK, TM), F32),
        ],
        compiler_params=_params(("arbitrary",)),
        name="outproj",
    )(attn_ctx, attn_lat, cm, bg, zc, zc, zc, x, mod, w_out, conv_w, g2, wp_hi, wp_lo, sub_keys)


GELU_C0 = math.sqrt(2.0 / math.pi)
GELU_C1 = GELU_C0 * 0.044715


def _gated_gelu(half_gate, x):
    hx = half_gate * x
    return hx * jnp.tanh(x * (GELU_C0 + GELU_C1 * (x * x))) + hx


def _peer_kernel(h2_ref, b1_ref, b2_ref, tau_ref, u_ref, vt_ref, o_ref, *scratch):
    a_scrs, w_scrs = scratch[:PEER_SUB], scratch[PEER_SUB:]
    c = pl.program_id(1)
    tmp = h2_ref.shape[0]
    ec = u_ref.shape[0]
    keys_per_step = ec // N_KEYS

    @pl.when(c == 0)
    def _():
        o_ref[...] = jnp.zeros_like(o_ref)

    n_sub = len(a_scrs)
    es = ec // n_sub
    keys_per_sub = es // N_KEYS
    def scores(s):
        a_scrs[s][...] = _dot_nt(u_ref[s * es:(s + 1) * es, :], h2_ref[...])

    scores(0)
    for s in range(n_sub):
        if s + 1 < n_sub:
            scores(s + 1)
        for j in range(keys_per_sub):
            i1 = c * keys_per_step + s * keys_per_sub + j
            rows = slice(j * N_KEYS, (j + 1) * N_KEYS)
            for tb in range(tmp // LANES):
                tl = slice(tb * LANES, (tb + 1) * LANES)
                gate = None
                for h in range(PEER_HEADS):
                    row = b1_ref[h, tb, pl.ds(i1, SUBLANES, stride=0), :]
                    row = jnp.broadcast_to(row[None], (N_KEYS // SUBLANES, SUBLANES, LANES)).reshape(N_KEYS, LANES)
                    t = row + b2_ref[h, :, tl]
                    g = jnp.where(t >= tau_ref[h, :, tl], jnp.exp2(t), 0.0)
                    gate = g if gate is None else gate + g
                w_scrs[s][rows, tl] = _gated_gelu(gate, a_scrs[s][rows, tl]).astype(BF16)
        o_ref[...] += _dot(vt_ref[:, s * es:(s + 1) * es], w_scrs[s][...])


def _peer(h2, b1, b2, tau, u_bf, vt_bf, dims):
    n_tok = dims["n_tok"]
    return pl.pallas_call(
        _peer_kernel,
        grid=(n_tok // TMP, N_EXPERTS // EC),
        in_specs=[
            pl.BlockSpec((TMP, D_MODEL), lambda t, c: (t, 0), pipeline_mode=pl.Buffered(1)),
            pl.BlockSpec((PEER_HEADS, TMP // LANES, N_KEYS, LANES), lambda t, c: (0, t, 0, 0),
                         pipeline_mode=pl.Buffered(1)),
            pl.BlockSpec((PEER_HEADS, N_KEYS, TMP), lambda t, c: (0, 0, t), pipeline_mode=pl.Buffered(1)),
            pl.BlockSpec((PEER_HEADS, 1, TMP), lambda t, c: (0, 0, t), pipeline_mode=pl.Buffered(1)),
            pl.BlockSpec((EC, D_MODEL), lambda t, c: (c, 0)),
            pl.BlockSpec((None, D_MODEL, EC), lambda t, c: (c, 0, 0)),
        ],
        out_specs=pl.BlockSpec((D_MODEL, TMP), lambda t, c: (0, t)),
        out_shape=jax.ShapeDtypeStruct((D_MODEL, n_tok), F32),
        scratch_shapes=([pltpu.VMEM((EC // PEER_SUB, TMP), F32)] * PEER_SUB
                        + [pltpu.VMEM((EC // PEER_SUB, TMP), BF16)] * PEER_SUB),
        compiler_params=_params(("arbitrary", "arbitrary")),
        name="peer",
    )(h2, b1, b2, tau, u_bf, vt_bf)


def _final_kernel(x_ref, ft_ref, mod_ref, g_ref, o_ref):
    x = x_ref[...] + mod_ref[0, 5:6, :] * ft_ref[...].T
    o_ref[...] = _rms_rows(x) * g_ref[...]


def _final(x1, ffn_t, mod, g_final, row0, n_rows, mod_row):
    off = row0 // TM
    return pl.pallas_call(
        _final_kernel,
        grid=(n_rows // TM,),
        in_specs=[
            pl.BlockSpec((TM, D_MODEL), lambda i: (off + i, 0)),
            pl.BlockSpec((D_MODEL, TM), lambda i: (0, off + i)),
            pl.BlockSpec((1, N_MOD, D_MODEL), lambda i: (mod_row(i), 0, 0)),
            pl.BlockSpec((1, D_MODEL), lambda i: (0, 0)),
        ],
        out_specs=pl.BlockSpec((TM, D_MODEL), lambda i: (i, 0)),
        out_shape=jax.ShapeDtypeStruct((n_rows, D_MODEL), F32),
        compiler_params=_params(("arbitrary",)),
        name="final",
    )(x1, ffn_t, mod, g_final)


def _rope_tables(n_tokens):
    rows = n_tokens // GRID_W
    t = jnp.arange(rows * GRID_W)
    pos = jnp.stack([t // GRID_W, t % GRID_W], axis=-1).astype(F32)
    n_freq = HEAD_DIM // 4
    inv = ROPE_THETA ** (-jnp.arange(n_freq, dtype=F32) / n_freq)
    ang = pos[:, :, None] * inv
    ang = jnp.broadcast_to(ang[:, :, None, :], (ang.shape[0], 2, 2, n_freq)).reshape(-1, HEAD_DIM)
    cos, sin = jnp.cos(ang), jnp.sin(ang)
    first = (jnp.arange(HEAD_DIM) % (2 * n_freq)) < n_freq
    return cos, jnp.where(first, -sin, 0.0), jnp.where(first, 0.0, sin)


def kernel(x_prompt, x_sample, cache_k, cache_v, c, c_ctx, w_ada, b_ada, g_norm1, g_norm2, w_in, q_gain, k_gain,
           w_s, b_s, conv_w, w_out, w_pq, sub_keys, u_tab, v_tab, g_final):
    batch, seq, _ = x_prompt.shape
    dec_batch, dec_seq, _ = x_sample.shape
    depth = w_in.shape[0]
    past = cache_k.shape[2]
    n_ctx = batch * seq
    n_tok = n_ctx + dec_batch * dec_seq
    dims = dict(batch=batch, seq=seq, dec_batch=dec_batch, dec_seq=dec_seq, n_ctx=n_ctx, n_tok=n_tok)
    assert seq % TM == 0 and dec_seq % TM == 0 and n_ctx % dec_seq == 0 and n_tok % TMP == 0
    assert 1 + dec_batch <= COND_ROWS

    x = jnp.concatenate([x_prompt.reshape(n_ctx, D_MODEL), x_sample.reshape(dec_batch * dec_seq, D_MODEL)], axis=0)
    cond = jnp.zeros((COND_ROWS, D_MODEL), F32).at[0].set(c_ctx).at[1:1 + dec_batch].set(c)
    mod_all = _ada(cond, w_ada, b_ada).reshape(depth, COND_ROWS, N_MOD, D_MODEL)
    cos, sa, sb = _rope_tables(dec_seq)

    tps_lat = dec_seq // TM
    n_ctx_tiles = n_ctx // TM
    ffn_t = None
    mod_prev = None
    new_k, new_v = [], []
    for l in range(depth):
        mod = mod_all[l]
        w_in_bf = w_in[l].astype(BF16)
        w_out_bf = w_out[l].astype(BF16)
        wp_hi = w_pq[l].astype(BF16)
        wp_lo = (w_pq[l] - wp_hi.astype(F32)).astype(BF16)
        u_bf = u_tab[l].astype(BF16)
        vt_bf = v_tab[l].astype(BF16).reshape(N_EXPERTS // EC, EC, D_MODEL).transpose(0, 2, 1)
        ws_bf = w_s[l].astype(BF16)
        bs_b = jnp.broadcast_to(b_s[l][:, :, None], (CM_HEADS, CHUNK, CM_DIM))
        ck = cache_k[:, l].reshape(dec_batch, past, ATT_KV_W).astype(BF16)
        cv = cache_v[:, l].reshape(dec_batch, past, ATT_KV_W).astype(BF16)

        outs = _inproj(x, ffn_t, mod_prev, mod, g_norm1[l][None], w_in_bf, q_gain[l][None], k_gain[l][None],
                       cos, sa, sb, ws_bf, bs_b, dims)
        if ffn_t is not None:
            x = outs[0]
            outs = outs[1:]
        q, kr, vb, kf, vf, cm, bg, zc = outs
        new_k.append(kf.reshape(batch, seq, N_KV, HEAD_DIM))
        new_v.append(vf.reshape(batch, seq, N_KV, HEAD_DIM))
        attn_ctx, attn_lat = _attention(q, kr, vb, ck, cv, dims)
        x, h2, b1, b2, tau = _outproj(attn_ctx, attn_lat, cm, bg, zc, x, mod, w_out_bf, conv_w[l], g_norm2[l][None], wp_hi, wp_lo,
                                      sub_keys[l].reshape(N_GROUPS, N_KEYS, PK_HALF), dims)
        ffn_t = _peer(h2, b1, b2, tau, u_bf, vt_bf, dims)
        mod_prev = mod

    gf = g_final[None]
    y_prompt = _final(x, ffn_t, mod_prev, gf, 0, n_ctx, lambda i: 0)
    y_sample = _final(x, ffn_t, mod_prev, gf, n_ctx, n_tok - n_ctx, lambda i: 1 + i // tps_lat)
    return (y_prompt.reshape(batch, seq, D_MODEL), y_sample.reshape(dec_batch, dec_seq, D_MODEL),
            jnp.stack(new_k, axis=1), jnp.stack(new_v, axis=1))
```

```python
import functools
import math

import jax
import jax.numpy as jnp
from jax import lax
from jax.experimental import pallas as pl
from jax.experimental.pallas import tpu as pltpu

F32 = jnp.float32
BF16 = jnp.bfloat16

D_MODEL = 2048
GRID_W = 64
N_HEADS = 8
N_KV = 2
HEAD_DIM = 128
GROUP = N_HEADS // N_KV
ATT_Q_W = N_HEADS * HEAD_DIM
ATT_KV_W = N_KV * HEAD_DIM
ROPE_THETA = 10000.0
CHUNK = 128
CM_HEADS = 4
CM_DIM = 128
CM_W = CM_HEADS * CM_DIM
CV_W = 512
D_IN = ATT_Q_W + 2 * ATT_KV_W + 2 * CM_W + 3 * CV_W
PEER_HEADS = 8
N_KEYS = 128
N_EXPERTS = N_KEYS * N_KEYS
TOPK = 16
PK_HALF = 128
N_GROUPS = 2 * PEER_HEADS
N_MOD = 6
EPS = 1e-6
NEG_INF = float("-inf")
LOG2E = math.log2(math.e)
Q_SCALE = HEAD_DIM ** -0.5 * LOG2E

LANES = 128
SUBLANES = 8
VMEM_LIMIT = 56 * 1024 * 1024

TM = 256
TQ = 256
TMP = 1024
EC = 1024
PEER_SUB = 2
ADA_NT = 1024
COND_ROWS = 16

O_Q = 0
O_K = O_Q + ATT_Q_W
O_V = O_K + ATT_KV_W
O_UCM = O_V + ATT_KV_W
O_VCM = O_UCM + CM_W
O_BCV = O_VCM + CM_W
O_CCV = O_BCV + CV_W
O_HCV = O_CCV + CV_W


def _dot(a, b):
    return jnp.dot(a, b, preferred_element_type=F32)


def _dot_nt(a, b):
    return lax.dot_general(a, b, (((1,), (1,)), ((), ())), preferred_element_type=F32)


def _split(x):
    hi = x.astype(BF16)
    lo = (x - hi.astype(F32)).astype(BF16)
    return hi, lo


def _rms_rows(x):
    return x * lax.rsqrt(jnp.mean(x * x, axis=-1, keepdims=True) + EPS)


def _params(sem):
    return pltpu.CompilerParams(dimension_semantics=sem, vmem_limit_bytes=VMEM_LIMIT)


def _ada_kernel(c_ref, w_ref, b_ref, o_ref):
    c = c_ref[...]
    a = c * jax.nn.sigmoid(c)
    a_hi, a_lo = _split(a)
    w_hi, w_lo = _split(w_ref[0])
    o_ref[0] = _dot(a_hi, w_hi) + _dot(a_lo, w_hi) + _dot(a_hi, w_lo) + b_ref[0]


def _ada(cond, w_ada, b_ada):
    depth, _, n_out = w_ada.shape
    return pl.pallas_call(
        _ada_kernel,
        grid=(depth, n_out // ADA_NT),
        in_specs=[
            pl.BlockSpec((COND_ROWS, D_MODEL), lambda l, j: (0, 0)),
            pl.BlockSpec((1, D_MODEL, ADA_NT), lambda l, j: (l, 0, j)),
            pl.BlockSpec((1, 1, ADA_NT), lambda l, j: (l, 0, j)),
        ],
        out_specs=pl.BlockSpec((1, COND_ROWS, ADA_NT), lambda l, j: (l, 0, j)),
        out_shape=jax.ShapeDtypeStruct((depth, COND_ROWS, n_out), F32),
        compiler_params=_params(("arbitrary", "arbitrary")),
        name="ada",
    )(cond, w_ada, b_ada.reshape(depth, 1, n_out))


def _inproj_kernel(*refs, has_ffn, n_ctx_tiles):
    refs = list(refs)
    x_ref = refs.pop(0)
    if has_ffn:
        ft_ref = refs.pop(0)
        modp_ref = refs.pop(0)
    (mod_ref, g1_ref, w_ref, qg_ref, kg_ref, cos_ref, sa_ref, sb_ref, ws_ref, bs_ref) = refs[:10]
    outs = refs[10:]
    if has_ffn:
        xo_ref = outs.pop(0)
    q_ref, kr_ref, vb_ref, kf_ref, vf_ref, cm_ref, bg_ref, zc_ref = outs

    i = pl.program_id(0)
    is_lat = i >= n_ctx_tiles
    x = x_ref[...]
    if has_ffn:
        x = x + modp_ref[0, 5:6, :] * ft_ref[...].T
        xo_ref[...] = x
    sh1 = mod_ref[0, 0:1, :]
    sc1 = mod_ref[0, 1:2, :]
    h = _rms_rows(x) * g1_ref[...]
    hb = (h * (1.0 + sc1) + sh1).astype(BF16)

    cos = jnp.where(is_lat, cos_ref[...], 1.0)
    sa = jnp.where(is_lat, sa_ref[...], 0.0)
    sb = jnp.where(is_lat, sb_ref[...], 0.0)

    def rope(y):
        return y * cos + pltpu.roll(y, 96, 1) * sa + pltpu.roll(y, 32, 1) * sb

    zq = _dot(hb, w_ref[:, O_Q:O_K])
    for hd in range(N_HEADS):
        sl = slice(hd * HEAD_DIM, (hd + 1) * HEAD_DIM)
        y = _rms_rows(zq[:, sl]) * qg_ref[...]
        q_ref[:, sl] = (rope(y) * Q_SCALE).astype(BF16)

    zk = _dot(hb, w_ref[:, O_K:O_V])
    zv = _dot(hb, w_ref[:, O_V:O_UCM])
    vb_ref[...] = zv.astype(BF16)
    kn = []
    for hd in range(N_KV):
        sl = slice(hd * HEAD_DIM, (hd + 1) * HEAD_DIM)
        y = _rms_rows(zk[:, sl]) * kg_ref[...]
        kn.append(y)
        kr_ref[:, sl] = rope(y).astype(BF16)

    @pl.when(jnp.logical_not(is_lat))
    def _():
        for hd in range(N_KV):
            kf_ref[:, hd * HEAD_DIM:(hd + 1) * HEAD_DIM] = kn[hd]
        vf_ref[...] = zv

    zu = _dot(hb, w_ref[:, O_UCM:O_VCM])
    zvc = _dot(hb, w_ref[:, O_VCM:O_BCV])
    n_chunks = zu.shape[0] // CHUNK
    for c in range(CM_HEADS):
        sl = slice(c * CM_DIM, (c + 1) * CM_DIM)
        vh = _rms_rows(zvc[:, sl]).astype(BF16)
        for n in range(n_chunks):
            rows = slice(n * CHUNK, (n + 1) * CHUNK)
            mixed = _dot(ws_ref[c], vh[rows]) + bs_ref[c]
            cm_ref[rows, sl] = (zu[rows, sl] * mixed).astype(BF16)

    bg_ref[...] = _dot(hb, w_ref[:, O_BCV:O_CCV])
    zc_ref[...] = _dot(hb, w_ref[:, O_CCV:O_HCV]) * _dot(hb, w_ref[:, O_HCV:D_IN])


def _inproj(x, ffn_t, mod_prev, mod, g1, w_in, q_gain, k_gain, cos, sa, sb, w_s, b_s, dims):
    n_tok, n_ctx, tps_lat = dims["n_tok"], dims["n_ctx"], dims["dec_seq"] // TM
    n_ctx_tiles = n_ctx // TM
    has_ffn = ffn_t is not None

    def mod_row(i):
        return jnp.where(i < n_ctx_tiles, 0, 1 + jnp.maximum(i - n_ctx_tiles, 0) // tps_lat)

    def pos_blk(i):
        return jnp.maximum(i - n_ctx_tiles, 0) % tps_lat

    def ctx_blk(i):
        return jnp.minimum(i, n_ctx_tiles - 1)

    tok = lambda w: pl.BlockSpec((TM, w), lambda i: (i, 0))
    full = lambda a: pl.BlockSpec(a.shape, lambda i: (0,) * a.ndim)
    mod_spec = pl.BlockSpec((1, N_MOD, D_MODEL), lambda i: (mod_row(i), 0, 0))
    rope_spec = pl.BlockSpec((TM, HEAD_DIM), lambda i: (pos_blk(i), 0))

    args, in_specs = [x], [tok(D_MODEL)]
    if has_ffn:
        args += [ffn_t, mod_prev]
        in_specs += [pl.BlockSpec((D_MODEL, TM), lambda i: (0, i)), mod_spec]
    args += [mod, g1, w_in, q_gain, k_gain, cos, sa, sb, w_s, b_s]
    in_specs += [mod_spec, full(g1), full(w_in), full(q_gain), full(k_gain),
                 rope_spec, rope_spec, rope_spec, full(w_s), full(b_s)]

    out_shape, out_specs = [], []
    if has_ffn:
        out_shape.append(jax.ShapeDtypeStruct((n_tok, D_MODEL), F32))
        out_specs.append(tok(D_MODEL))
    ctx_spec = pl.BlockSpec((TM, ATT_KV_W), lambda i: (ctx_blk(i), 0))
    out_shape += [
        jax.ShapeDtypeStruct((n_tok, ATT_Q_W), BF16),
        jax.ShapeDtypeStruct((n_tok, ATT_KV_W), BF16),
        jax.ShapeDtypeStruct((n_tok, ATT_KV_W), BF16),
        jax.ShapeDtypeStruct((n_ctx, ATT_KV_W), F32),
        jax.ShapeDtypeStruct((n_ctx, ATT_KV_W), F32),
        jax.ShapeDtypeStruct((n_tok, CM_W), BF16),
        jax.ShapeDtypeStruct((n_tok, CV_W), F32),
        jax.ShapeDtypeStruct((n_tok, CV_W), F32),
    ]
    out_specs += [tok(ATT_Q_W), tok(ATT_KV_W), tok(ATT_KV_W), ctx_spec, ctx_spec,
                  tok(CM_W), tok(CV_W), tok(CV_W)]
    return pl.pallas_call(
        functools.partial(_inproj_kernel, has_ffn=has_ffn, n_ctx_tiles=n_ctx_tiles),
        grid=(n_tok // TM,),
        in_specs=in_specs,
        out_specs=out_specs,
        out_shape=out_shape,
        compiler_params=_params(("arbitrary",)),
        name="inproj",
    )(*args)


def _attend(q_ref, parts, o_ref):
    t = q_ref.shape[0]
    for kv in range(N_KV):
        ksl = slice(kv * HEAD_DIM, (kv + 1) * HEAD_DIM)
        qs = jnp.concatenate(
            [q_ref[:, (kv * GROUP + g) * HEAD_DIM:(kv * GROUP + g + 1) * HEAD_DIM] for g in range(GROUP)], axis=0)
        scores = [_dot_nt(qs, k_ref[:, ksl]) for k_ref, _ in parts]
        m = scores[0].max(axis=-1, keepdims=True)
        for s in scores[1:]:
            m = jnp.maximum(m, s.max(axis=-1, keepdims=True))
        den = None
        num = None
        for s, (_, v_ref) in zip(scores, parts):
            p = jnp.exp2(s - m)
            ps = p.sum(axis=-1, keepdims=True)
            pv = _dot(p.astype(BF16), v_ref[:, ksl])
            den = ps if den is None else den + ps
            num = pv if num is None else num + pv
        o = num / den
        for g in range(GROUP):
            hd = kv * GROUP + g
            o_ref[:, hd * HEAD_DIM:(hd + 1) * HEAD_DIM] = o[g * t:(g + 1) * t].astype(BF16)


def _attn_ctx_kernel(q_ref, k_ref, v_ref, o_ref):
    _attend(q_ref, [(k_ref, v_ref)], o_ref)


def _attn_lat_kernel(q_ref, k_ref, v_ref, ck_ref, cv_ref, o_ref):
    _attend(q_ref, [(k_ref, v_ref), (ck_ref, cv_ref)], o_ref)


def _attention(q, kr, vb, ck, cv, dims):
    n_tok, n_ctx = dims["n_tok"], dims["n_ctx"]
    batch, seq, dec_batch, dec_seq = dims["batch"], dims["seq"], dims["dec_batch"], dims["dec_seq"]
    past = ck.shape[1]
    attn_ctx = pl.pallas_call(
        _attn_ctx_kernel,
        grid=(batch,),
        in_specs=[
            pl.BlockSpec((seq, ATT_Q_W), lambda b: (b, 0)),
            pl.BlockSpec((seq, ATT_KV_W), lambda b: (b, 0)),
            pl.BlockSpec((seq, ATT_KV_W), lambda b: (b, 0)),
        ],
        out_specs=pl.BlockSpec((seq, ATT_Q_W), lambda b: (b, 0)),
        out_shape=jax.ShapeDtypeStruct((n_ctx, ATT_Q_W), BF16),
        compiler_params=_params(("arbitrary",)),
        name="attn_ctx",
    )(q, kr, vb)
    nq = dec_seq // TQ
    q_off = n_ctx // TQ
    kv_off = n_ctx // dec_seq
    attn_lat = pl.pallas_call(
        _attn_lat_kernel,
        grid=(dec_batch, nq),
        in_specs=[
            pl.BlockSpec((TQ, ATT_Q_W), lambda b, j: (q_off + b * nq + j, 0)),
            pl.BlockSpec((dec_seq, ATT_KV_W), lambda b, j: (kv_off + b, 0)),
            pl.BlockSpec((dec_seq, ATT_KV_W), lambda b, j: (kv_off + b, 0)),
            pl.BlockSpec((None, past, ATT_KV_W), lambda b, j: (b, 0, 0)),
            pl.BlockSpec((None, past, ATT_KV_W), lambda b, j: (b, 0, 0)),
        ],
        out_specs=pl.BlockSpec((TQ, ATT_Q_W), lambda b, j: (b * nq + j, 0)),
        out_shape=jax.ShapeDtypeStruct((n_tok - n_ctx, ATT_Q_W), BF16),
        compiler_params=_params(("arbitrary", "arbitrary")),
        name="attn_lat",
    )(q, kr, vb, ck, cv)
    return attn_ctx, attn_lat


def _extract_top(vals, idxs, n):
    big = jnp.int32(1 << 30)
    maxima = []
    for _ in range(n):
        m = functools.reduce(jnp.maximum, vals).max(axis=0, keepdims=True)
        cand = [jnp.where(v == m, ix, big) for v, ix in zip(vals, idxs)]
        mi = functools.reduce(jnp.minimum, cand).min(axis=0, keepdims=True)
        vals = [jnp.where(ix == mi, NEG_INF, v) for v, ix in zip(vals, idxs)]
        maxima.append(m)
    return maxima, vals


def _oddeven_merge_sort_pairs(lo, hi):
    def merge(lo, hi, r):
        step = r * 2
        if step < hi - lo:
            yield from merge(lo, hi, step)
            yield from merge(lo + r, hi, step)
            for i in range(lo + r, hi - r, step):
                yield (i, i + r)
        else:
            yield (lo, lo + r)

    if hi - lo >= 1:
        mid = lo + (hi - lo) // 2
        yield from _oddeven_merge_sort_pairs(lo, mid)
        yield from _oddeven_merge_sort_pairs(mid + 1, hi)
        yield from merge(lo, hi, 1)


_SORT16 = tuple(_oddeven_merge_sort_pairs(0, TOPK - 1))


def _compare_exchange(v, i, j):
    if v[j] is None:
        return
    if v[i] is None:
        v[i], v[j] = v[j], None
        return
    v[i], v[j] = jnp.maximum(v[i], v[j]), jnp.minimum(v[i], v[j])


def _merge_sublanes(v):
    n = len(v)
    for shift in (4, 2, 1):
        y = [None if v[n - 1 - a] is None else pltpu.roll(v[n - 1 - a], shift, 0) for a in range(n)]
        v = [y[a] if v[a] is None else (v[a] if y[a] is None else jnp.maximum(v[a], y[a])) for a in range(n)]
        for d in (8, 4, 2, 1):
            for i in range(n):
                if not i & d:
                    _compare_exchange(v, i, i + d)
    return v


def _top16_sorted(s):
    blocks = [s[SUBLANES * a:SUBLANES * (a + 1), :] for a in range(N_KEYS // SUBLANES)]
    v = list(blocks)
    for i, j in _SORT16:
        _compare_exchange(v, i, j)
    v = _merge_sublanes(v)
    count = functools.reduce(jnp.add, [jnp.where(b >= v[TOPK - 1], 1.0, 0.0) for b in blocks]).sum(axis=0, keepdims=True)
    return v, count


def _stack_rows(v, r8):
    halves = []
    for base in (0, SUBLANES):
        acc = v[base]
        for k in range(1, SUBLANES):
            acc = jnp.where(r8 == k, v[base + k], acc)
        halves.append(acc)
    return jnp.concatenate(halves, axis=0)


def _kth_largest(vals):
    chain = [vals[0]] + list(vals[2:9])
    for extra in (vals[1], vals[9]):
        x = extra
        for k in range(len(chain)):
            chain[k], x = jnp.maximum(chain[k], x), jnp.minimum(chain[k], x)
        chain.append(x)
    v = _merge_sublanes(chain + [None] * (TOPK - len(chain)))
    return v[TOPK - 1][0:1]


def _pair_blocks(m1, m2, r8):
    vals = [m1[0:1] + m2[0:8], m1[0:1] + m2[8:16], m1[1:2] + m2[0:8]]
    idxs = [r8, r8 + 8, r8 + TOPK]
    for i in range(2, 8):
        vals.append(jnp.where(r8 < TOPK // (i + 1), m1[i:i + 1] + m2[0:8], NEG_INF))
        idxs.append(r8 + TOPK * i)
    vals.append(m1[8:16] + m2[0:1])
    idxs.append((r8 + 8) * TOPK)
    return vals, idxs


def _outproj_kernel(attn_c_ref, attn_l_ref, cm_ref, bg_ref, zc_ref, zp_ref, zn_ref, x_ref, mod_ref, wo_ref, cw_ref, g2_ref,
                    wp_ref, sk_ref,
                    x1_ref, h2_ref, b1_ref, b2_ref, tau_ref,
                    zpad_scr, st_scr, m_scr, *, n_ctx_tiles, tps_ctx, tps_lat):
    i = pl.program_id(0)
    tm = x_ref.shape[0]
    is_lat = i >= n_ctx_tiles
    jj = jnp.where(is_lat, jnp.maximum(i - n_ctx_tiles, 0) % tps_lat, i % tps_ctx)
    nt = jnp.where(is_lat, tps_lat, tps_ctx)

    zc = zc_ref[...]
    zpad_scr[0:SUBLANES, :] = jnp.where(jj == 0, 0.0, zp_ref[...])
    zpad_scr[SUBLANES:SUBLANES + tm, :] = zc
    zpad_scr[SUBLANES + tm:2 * SUBLANES + tm, :] = jnp.where(jj == nt - 1, 0.0, zn_ref[...])
    zprev = zpad_scr[SUBLANES - 1:SUBLANES - 1 + tm, :]
    znext = zpad_scr[SUBLANES + 1:SUBLANES + 1 + tm, :]
    y = zprev * cw_ref[0:1, :] + zc * cw_ref[1:2, :] + znext * cw_ref[2:3, :]
    cv = (bg_ref[...] * y).astype(BF16)

    attn = jnp.where(is_lat, attn_l_ref[...], attn_c_ref[...])
    mix = _dot(jnp.concatenate([attn, cm_ref[...], cv], axis=1), wo_ref[...])
    g1 = mod_ref[0, 2:3, :]
    sh2 = mod_ref[0, 3:4, :]
    sc2 = mod_ref[0, 4:5, :]
    x1 = x_ref[...] + g1 * mix
    x1_ref[...] = x1
    h2 = (_rms_rows(x1) * g2_ref[...]) * (1.0 + sc2) + sh2
    hb = h2.astype(BF16)
    h2_ref[...] = hb

    qp = _dot(hb, wp_ref[...]).astype(BF16)
    for g in range(N_GROUPS):
        st_scr[g] = _dot_nt(sk_ref[g], qp[:, g * PK_HALF:(g + 1) * PK_HALF])

    n_lt = tm // LANES
    r8 = lax.broadcasted_iota(jnp.int32, (SUBLANES, LANES), 0)

    def finish(h, lt, lanes, a1, a2, top, picked, vals, m1, m2):
        z = functools.reduce(
            jnp.add, [jnp.where(p, jnp.exp(v - top), 0.0) for p, v in zip(picked, vals)]).sum(axis=0, keepdims=True)
        lz = jnp.log(z)
        shifted = _pair_blocks(((m1 - top) - lz) * LOG2E - 1.0, m2 * LOG2E, r8)[0]
        tau = functools.reduce(
            jnp.minimum, [jnp.where(p, v, jnp.inf) for p, v in zip(picked, shifted)]).min(axis=0, keepdims=True)
        tau_ref[h, :, lanes] = tau
        b1_ref[h, lt] = ((a1 - top) - lz) * LOG2E - 1.0
        b2_ref[h, :, lanes] = a2 * LOG2E

    def fast(h, bad):
        for lt in range(n_lt):
            lanes = slice(lt * LANES, (lt + 1) * LANES)
            s1 = st_scr[2 * h, :, lanes]
            s2 = st_scr[2 * h + 1, :, lanes]
            v1, n1 = _top16_sorted(s1)
            v2, n2 = _top16_sorted(s2)
            m1 = _stack_rows(v1, r8)
            m2 = _stack_rows(v2, r8)
            top = m1[0:1] + m2[0:1]
            vals = _pair_blocks(m1, m2, r8)[0]
            thr = _kth_largest(vals)
            picked = [v >= thr for v in vals]
            n12 = functools.reduce(jnp.add, [jnp.where(p, 1.0, 0.0) for p in picked]).sum(axis=0, keepdims=True)
            bad = jnp.maximum(bad, jnp.where((n1 == TOPK) & (n2 == TOPK) & (n12 == TOPK), 0.0, 1.0))
            a1 = jnp.where(s1 >= v1[TOPK - 1][0:1], s1, NEG_INF)
            a2 = jnp.where(s2 >= v2[TOPK - 1][0:1], s2, NEG_INF)
            finish(h, lt, lanes, a1, a2, top, picked, vals, m1, m2)
        return bad

    bad = lax.fori_loop(0, PEER_HEADS, fast, jnp.zeros((1, LANES), F32))

    @pl.when(jnp.max(bad) > 0.0)
    def _():
        def stage1(g, carry):
            for lt in range(n_lt):
                lanes = slice(lt * LANES, (lt + 1) * LANES)
                s = st_scr[g, :, lanes]
                rows = lax.broadcasted_iota(jnp.int32, s.shape, 0)
                maxima, left = _extract_top([s], [rows], TOPK)
                m_scr[g, :, lanes] = jnp.concatenate(maxima, axis=0)
                st_scr[g, :, lanes] = jnp.where(left[0] == s, NEG_INF, s)
            return carry

        lax.fori_loop(0, N_GROUPS, stage1, 0)

        def stage2(h, carry):
            for lt in range(n_lt):
                lanes = slice(lt * LANES, (lt + 1) * LANES)
                m1 = m_scr[2 * h, :, lanes]
                m2 = m_scr[2 * h + 1, :, lanes]
                top = m1[0:1] + m2[0:1]
                vals, idxs = _pair_blocks(m1, m2, r8)
                _, left = _extract_top(vals, idxs, TOPK)
                picked = [l != v for l, v in zip(left, vals)]
                finish(h, lt, lanes, st_scr[2 * h, :, lanes], st_scr[2 * h + 1, :, lanes], top, picked, vals, m1, m2)
            return carry

        lax.fori_loop(0, PEER_HEADS, stage2, 0)


def _outproj(attn_ctx, attn_lat, cm, bg, zc, x, mod, w_out, conv_w, g2, w_pq, sub_keys, dims):
    n_tok, n_ctx = dims["n_tok"], dims["n_ctx"]
    n_ctx_tiles = n_ctx // TM
    n_tiles = n_tok // TM
    tps_lat = dims["dec_seq"] // TM
    tps_ctx = dims["seq"] // TM
    halo_per_tile = TM // SUBLANES
    n_halo = n_tok // SUBLANES

    def mod_row(i):
        return jnp.where(i < n_ctx_tiles, 0, 1 + jnp.maximum(i - n_ctx_tiles, 0) // tps_lat)

    tok = lambda w: pl.BlockSpec((TM, w), lambda i: (i, 0))
    full = lambda a: pl.BlockSpec(a.shape, lambda i: (0,) * a.ndim)
    sel_spec = pl.BlockSpec((PEER_HEADS, N_KEYS, TM), lambda i: (0, 0, i))
    return pl.pallas_call(
        functools.partial(_outproj_kernel, n_ctx_tiles=n_ctx_tiles, tps_ctx=tps_ctx, tps_lat=tps_lat),
        grid=(n_tiles,),
        in_specs=[
            pl.BlockSpec((TM, ATT_Q_W), lambda i: (jnp.minimum(i, n_ctx_tiles - 1), 0)),
            pl.BlockSpec((TM, ATT_Q_W), lambda i: (jnp.maximum(i - n_ctx_tiles, 0), 0)),
            tok(CM_W), tok(CV_W), tok(CV_W),
            pl.BlockSpec((SUBLANES, CV_W), lambda i: (jnp.maximum(i * halo_per_tile - 1, 0), 0)),
            pl.BlockSpec((SUBLANES, CV_W), lambda i: (jnp.minimum((i + 1) * halo_per_tile, n_halo - 1), 0)),
            tok(D_MODEL),
            pl.BlockSpec((1, N_MOD, D_MODEL), lambda i: (mod_row(i), 0, 0)),
            full(w_out), full(conv_w), full(g2), full(w_pq), full(sub_keys),
        ],
        out_specs=[
            tok(D_MODEL), tok(D_MODEL),
            pl.BlockSpec((PEER_HEADS, TM // LANES, N_KEYS, LANES), lambda i: (0, i, 0, 0)),
            sel_spec,
            pl.BlockSpec((PEER_HEADS, 1, TM), lambda i: (0, 0, i)),
        ],
        out_shape=[
            jax.ShapeDtypeStruct((n_tok, D_MODEL), F32),
            jax.ShapeDtypeStruct((n_tok, D_MODEL), BF16),
            jax.ShapeDtypeStruct((PEER_HEADS, n_tok // LANES, N_KEYS, LANES), F32),
            jax.ShapeDtypeStruct((PEER_HEADS, N_KEYS, n_tok), F32),
            jax.ShapeDtypeStruct((PEER_HEADS, 1, n_tok), F32),
        ],
        scratch_shapes=[
            pltpu.VMEM((TM + 2 * SUBLANES, CV_W), F32),
            pltpu.VMEM((N_GROUPS, N_KEYS, TM), F32),
            pltpu.VMEM((N_GROUPS, TOPK, TM), F32),
        ],
        compiler_params=_params(("arbitrary",)),
        name="outproj",
    )(attn_ctx, attn_lat, cm, bg, zc, zc, zc, x, mod, w_out, conv_w, g2, w_pq, sub_keys)


GELU_C0 = math.sqrt(2.0 / math.pi)
GELU_C1 = GELU_C0 * 0.044715


def _gated_gelu(half_gate, x):
    hx = half_gate * x
    return hx * jnp.tanh(x * (GELU_C0 + GELU_C1 * (x * x))) + hx


def _peer_kernel(h2_ref, b1_ref, b2_ref, tau_ref, u_ref, vt_ref, o_ref, *scratch):
    a_scrs, w_scrs = scratch[:PEER_SUB], scratch[PEER_SUB:]
    c = pl.program_id(1)
    tmp = h2_ref.shape[0]
    ec = u_ref.shape[0]
    keys_per_step = ec // N_KEYS

    n_sub = len(a_scrs)
    es = ec // n_sub
    keys_per_sub = es // N_KEYS

    def scores(s):
        a_scrs[s][...] = _dot_nt(u_ref[s * es:(s + 1) * es, :], h2_ref[...])

    def mix(s):
        o_ref[...] += _dot(vt_ref[:, s * es:(s + 1) * es], w_scrs[s][...])

    def gates(s):
        for j in range(keys_per_sub):
            i1 = c * keys_per_step + s * keys_per_sub + j
            rows = slice(j * N_KEYS, (j + 1) * N_KEYS)
            for tb in range(tmp // LANES):
                tl = slice(tb * LANES, (tb + 1) * LANES)
                gate = None
                for h in range(PEER_HEADS):
                    row = b1_ref[h, tb, pl.ds(i1, SUBLANES, stride=0), :]
                    row = jnp.broadcast_to(row[None], (N_KEYS // SUBLANES, SUBLANES, LANES)).reshape(N_KEYS, LANES)
                    t = row + b2_ref[h, :, tl]
                    g = jnp.where(t >= tau_ref[h, :, tl], jnp.exp2(t), 0.0)
                    gate = g if gate is None else gate + g
                w_scrs[s][rows, tl] = _gated_gelu(gate, a_scrs[s][rows, tl]).astype(BF16)

    @pl.when(c == 0)
    def _():
        o_ref[...] = jnp.zeros_like(o_ref)

    scores(0)
    for s in range(n_sub):
        if s + 1 < n_sub:
            scores(s + 1)
        gates(s)
        mix(s)


def _peer(h2, b1, b2, tau, u_bf, vt_bf, dims):
    n_tok = dims["n_tok"]
    n_chunks = N_EXPERTS // EC
    return pl.pallas_call(
        _peer_kernel,
        grid=(n_tok // TMP, n_chunks),
        in_specs=[
            pl.BlockSpec((TMP, D_MODEL), lambda t, c: (t, 0), pipeline_mode=pl.Buffered(1)),
            pl.BlockSpec((PEER_HEADS, TMP // LANES, N_KEYS, LANES), lambda t, c: (0, t, 0, 0),
                         pipeline_mode=pl.Buffered(1)),
            pl.BlockSpec((PEER_HEADS, N_KEYS, TMP), lambda t, c: (0, 0, t), pipeline_mode=pl.Buffered(1)),
            pl.BlockSpec((PEER_HEADS, 1, TMP), lambda t, c: (0, 0, t), pipeline_mode=pl.Buffered(1)),
            pl.BlockSpec((EC, D_MODEL), lambda t, c: (c, 0)),
            pl.BlockSpec((None, D_MODEL, EC), lambda t, c: (c, 0, 0)),
        ],
        out_specs=pl.BlockSpec((D_MODEL, TMP), lambda t, c: (0, t)),
        out_shape=jax.ShapeDtypeStruct((D_MODEL, n_tok), F32),
        scratch_shapes=([pltpu.VMEM((EC // PEER_SUB, TMP), F32)] * PEER_SUB
                        + [pltpu.VMEM((EC // PEER_SUB, TMP), BF16)] * PEER_SUB),
        compiler_params=_params(("arbitrary", "arbitrary")),
        name="peer",
    )(h2, b1, b2, tau, u_bf, vt_bf)


def _final_kernel(x_ref, ft_ref, mod_ref, g_ref, o_ref):
    x = x_ref[...] + mod_ref[0, 5:6, :] * ft_ref[...].T
    o_ref[...] = _rms_rows(x) * g_ref[...]


def _final(x1, ffn_t, mod, g_final, row0, n_rows, mod_row):
    off = row0 // TM
    return pl.pallas_call(
        _final_kernel,
        grid=(n_rows // TM,),
        in_specs=[
            pl.BlockSpec((TM, D_MODEL), lambda i: (off + i, 0)),
            pl.BlockSpec((D_MODEL, TM), lambda i: (0, off + i)),
            pl.BlockSpec((1, N_MOD, D_MODEL), lambda i: (mod_row(i), 0, 0)),
            pl.BlockSpec((1, D_MODEL), lambda i: (0, 0)),
        ],
        out_specs=pl.BlockSpec((TM, D_MODEL), lambda i: (i, 0)),
        out_shape=jax.ShapeDtypeStruct((n_rows, D_MODEL), F32),
        compiler_params=_params(("arbitrary",)),
        name="final",
    )(x1, ffn_t, mod, g_final)


def _rope_tables(n_tokens):
    rows = n_tokens // GRID_W
    t = jnp.arange(rows * GRID_W)
    pos = jnp.stack([t // GRID_W, t % GRID_W], axis=-1).astype(F32)
    n_freq = HEAD_DIM // 4
    inv = ROPE_THETA ** (-jnp.arange(n_freq, dtype=F32) / n_freq)
    ang = pos[:, :, None] * inv
    ang = jnp.broadcast_to(ang[:, :, None, :], (ang.shape[0], 2, 2, n_freq)).reshape(-1, HEAD_DIM)
    cos, sin = jnp.cos(ang), jnp.sin(ang)
    first = (jnp.arange(HEAD_DIM) % (2 * n_freq)) < n_freq
    return cos, jnp.where(first, -sin, 0.0), jnp.where(first, 0.0, sin)


def kernel(x_prompt, x_sample, cache_k, cache_v, c, c_ctx, w_ada, b_ada, g_norm1, g_norm2, w_in, q_gain, k_gain,
           w_s, b_s, conv_w, w_out, w_pq, sub_keys, u_tab, v_tab, g_final):
    batch, seq, _ = x_prompt.shape
    dec_batch, dec_seq, _ = x_sample.shape
    depth = w_in.shape[0]
    past = cache_k.shape[2]
    n_ctx = batch * seq
    n_tok = n_ctx + dec_batch * dec_seq
    dims = dict(batch=batch, seq=seq, dec_batch=dec_batch, dec_seq=dec_seq, n_ctx=n_ctx, n_tok=n_tok)
    assert seq % TM == 0 and dec_seq % TM == 0 and n_ctx % dec_seq == 0 and n_tok % TMP == 0
    assert 1 + dec_batch <= COND_ROWS

    x = jnp.concatenate([x_prompt.reshape(n_ctx, D_MODEL), x_sample.reshape(dec_batch * dec_seq, D_MODEL)], axis=0)
    cond = jnp.zeros((COND_ROWS, D_MODEL), F32).at[0].set(c_ctx).at[1:1 + dec_batch].set(c)
    mod_all = _ada(cond, w_ada, b_ada).reshape(depth, COND_ROWS, N_MOD, D_MODEL)
    cos, sa, sb = _rope_tables(dec_seq)

    tps_lat = dec_seq // TM
    n_ctx_tiles = n_ctx // TM
    ffn_t = None
    mod_prev = None
    new_k, new_v = [], []
    for l in range(depth):
        mod = mod_all[l]
        w_in_bf = w_in[l].astype(BF16)
        w_out_bf = w_out[l].astype(BF16)
        w_pq_bf = w_pq[l].astype(BF16)
        sk_bf = sub_keys[l].reshape(N_GROUPS, N_KEYS, PK_HALF).astype(BF16)
        u_bf = u_tab[l].astype(BF16)
        vt_bf = v_tab[l].astype(BF16).reshape(N_EXPERTS // EC, EC, D_MODEL).transpose(0, 2, 1)
        ws_bf = w_s[l].astype(BF16)
        bs_b = jnp.broadcast_to(b_s[l][:, :, None], (CM_HEADS, CHUNK, CM_DIM))
        ck = cache_k[:, l].reshape(dec_batch, past, ATT_KV_W).astype(BF16)
        cv = cache_v[:, l].reshape(dec_batch, past, ATT_KV_W).astype(BF16)

        outs = _inproj(x, ffn_t, mod_prev, mod, g_norm1[l][None], w_in_bf, q_gain[l][None], k_gain[l][None],
                       cos, sa, sb, ws_bf, bs_b, dims)
        if ffn_t is not None:
            x = outs[0]
            outs = outs[1:]
        q, kr, vb, kf, vf, cm, bg, zc = outs
        new_k.append(kf.reshape(batch, seq, N_KV, HEAD_DIM))
        new_v.append(vf.reshape(batch, seq, N_KV, HEAD_DIM))
        attn_ctx, attn_lat = _attention(q, kr, vb, ck, cv, dims)
        x, h2, b1, b2, tau = _outproj(attn_ctx, attn_lat, cm, bg, zc, x, mod, w_out_bf, conv_w[l], g_norm2[l][None], w_pq_bf,
                                      sk_bf, dims)
        ffn_t = _peer(h2, b1, b2, tau, u_bf, vt_bf, dims)
        mod_prev = mod

    gf = g_final[None]
    y_prompt = _final(x, ffn_t, mod_prev, gf, 0, n_ctx, lambda i: 0)
    y_sample = _final(x, ffn_t, mod_prev, gf, n_ctx, n_tok - n_ctx, lambda i: 1 + i // tps_lat)
    return (y_prompt.reshape(batch, seq, D_MODEL), y_sample.reshape(dec_batch, dec_seq, D_MODEL),
            jnp.stack(new_k, axis=1), jnp.stack(new_v, axis=1))
```

```python
import functools
import math

import jax
import jax.numpy as jnp
from jax import lax
from jax.experimental import pallas as pl
from jax.experimental.pallas import tpu as pltpu

F32 = jnp.float32
BF16 = jnp.bfloat16

D_MODEL = 2048
GRID_W = 64
N_HEADS = 8
N_KV = 2
HEAD_DIM = 128
GROUP = N_HEADS // N_KV
ATT_Q_W = N_HEADS * HEAD_DIM
ATT_KV_W = N_KV * HEAD_DIM
ROPE_THETA = 10000.0
CHUNK = 128
CM_HEADS = 4
CM_DIM = 128
CM_W = CM_HEADS * CM_DIM
CV_W = 512
D_IN = ATT_Q_W + 2 * ATT_KV_W + 2 * CM_W + 3 * CV_W
PEER_HEADS = 8
N_KEYS = 128
N_EXPERTS = N_KEYS * N_KEYS
TOPK = 16
PK_HALF = 128
N_GROUPS = 2 * PEER_HEADS
N_MOD = 6
EPS = 1e-6
NEG_INF = float("-inf")
LOG2E = math.log2(math.e)
Q_SCALE = HEAD_DIM ** -0.5 * LOG2E

LANES = 128
SUBLANES = 8
VMEM_LIMIT = 56 * 1024 * 1024

TM = 256
TQ = 256
TMP = 1024
EC = 1024
PEER_SUB = 2
ADA_NT = 1024
COND_ROWS = 16

O_Q = 0
O_K = O_Q + ATT_Q_W
O_V = O_K + ATT_KV_W
O_UCM = O_V + ATT_KV_W
O_VCM = O_UCM + CM_W
O_BCV = O_VCM + CM_W
O_CCV = O_BCV + CV_W
O_HCV = O_CCV + CV_W


def _dot(a, b):
    return jnp.dot(a, b, preferred_element_type=F32)


def _dot_nt(a, b):
    return lax.dot_general(a, b, (((1,), (1,)), ((), ())), preferred_element_type=F32)


def _split(x):
    hi = x.astype(BF16)
    lo = (x - hi.astype(F32)).astype(BF16)
    return hi, lo


def _rms_rows(x):
    return x * lax.rsqrt(jnp.mean(x * x, axis=-1, keepdims=True) + EPS)


def _params(sem):
    return pltpu.CompilerParams(dimension_semantics=sem, vmem_limit_bytes=VMEM_LIMIT)


def _ada_kernel(c_ref, w_ref, b_ref, o_ref):
    c = c_ref[...]
    a = c * jax.nn.sigmoid(c)
    a_hi, a_lo = _split(a)
    w_hi, w_lo = _split(w_ref[0])
    o_ref[0] = _dot(a_hi, w_hi) + _dot(a_lo, w_hi) + _dot(a_hi, w_lo) + b_ref[0]


def _ada(cond, w_ada, b_ada):
    depth, _, n_out = w_ada.shape
    return pl.pallas_call(
        _ada_kernel,
        grid=(depth, n_out // ADA_NT),
        in_specs=[
            pl.BlockSpec((COND_ROWS, D_MODEL), lambda l, j: (0, 0)),
            pl.BlockSpec((1, D_MODEL, ADA_NT), lambda l, j: (l, 0, j)),
            pl.BlockSpec((1, 1, ADA_NT), lambda l, j: (l, 0, j)),
        ],
        out_specs=pl.BlockSpec((1, COND_ROWS, ADA_NT), lambda l, j: (l, 0, j)),
        out_shape=jax.ShapeDtypeStruct((depth, COND_ROWS, n_out), F32),
        compiler_params=_params(("arbitrary", "arbitrary")),
        name="ada",
    )(cond, w_ada, b_ada.reshape(depth, 1, n_out))


def _qk_kernel(w_ref, k_ref, o_ref):
    w_hi, w_lo = _split(w_ref[0])
    k_hi, k_lo = _split(k_ref[0, 0])
    o_ref[0] = (_dot_nt(w_hi, k_hi) + _dot_nt(w_lo, k_hi) + _dot_nt(w_hi, k_lo)).astype(BF16)


def _query_key_weights(w_pq, sub_keys):
    depth = w_pq.shape[0]
    return pl.pallas_call(
        _qk_kernel,
        grid=(depth, N_GROUPS),
        in_specs=[
            pl.BlockSpec((1, D_MODEL, PK_HALF), lambda l, g: (l, 0, g)),
            pl.BlockSpec((1, 1, N_KEYS, PK_HALF), lambda l, g: (l, g, 0, 0)),
        ],
        out_specs=pl.BlockSpec((1, D_MODEL, N_KEYS), lambda l, g: (l, 0, g)),
        out_shape=jax.ShapeDtypeStruct((depth, D_MODEL, N_GROUPS * N_KEYS), BF16),
        compiler_params=_params(("arbitrary", "arbitrary")),
        name="qk_weights",
    )(w_pq, sub_keys.reshape(depth, N_GROUPS, N_KEYS, PK_HALF))


def _inproj_kernel(*refs, has_ffn, n_ctx_tiles):
    refs = list(refs)
    xa_ref = refs.pop(0)
    xb_ref = refs.pop(0)
    if has_ffn:
        ft_ref = refs.pop(0)
        modp_ref = refs.pop(0)
    (mod_ref, g1_ref, w_ref, qg_ref, kg_ref, cos_ref, sa_ref, sb_ref, ws_ref, bs_ref) = refs[:10]
    outs = refs[10:]
    if has_ffn:
        xo_ref = outs.pop(0)
    q_ref, kr_ref, vb_ref, kf_ref, vf_ref, cm_ref, bg_ref, zc_ref = outs

    i = pl.program_id(0)
    is_lat = i >= n_ctx_tiles
    x = jnp.where(is_lat, xb_ref[...], xa_ref[...])
    if has_ffn:
        x = x + modp_ref[0, 5:6, :] * ft_ref[...].T
        xo_ref[...] = x
    sh1 = mod_ref[0, 0:1, :]
    sc1 = mod_ref[0, 1:2, :]
    h = _rms_rows(x) * g1_ref[...]
    hb = (h * (1.0 + sc1) + sh1).astype(BF16)

    cos = jnp.where(is_lat, cos_ref[...], 1.0)
    sa = jnp.where(is_lat, sa_ref[...], 0.0)
    sb = jnp.where(is_lat, sb_ref[...], 0.0)

    def rope(y):
        return y * cos + pltpu.roll(y, 96, 1) * sa + pltpu.roll(y, 32, 1) * sb

    zq = _dot(hb, w_ref[:, O_Q:O_K])
    for hd in range(N_HEADS):
        sl = slice(hd * HEAD_DIM, (hd + 1) * HEAD_DIM)
        y = _rms_rows(zq[:, sl]) * qg_ref[...]
        q_ref[:, sl] = (rope(y) * Q_SCALE).astype(BF16)

    zk = _dot(hb, w_ref[:, O_K:O_V])
    zv = _dot(hb, w_ref[:, O_V:O_UCM])
    vb_ref[...] = zv.astype(BF16)
    kn = []
    for hd in range(N_KV):
        sl = slice(hd * HEAD_DIM, (hd + 1) * HEAD_DIM)
        y = _rms_rows(zk[:, sl]) * kg_ref[...]
        kn.append(y)
        kr_ref[:, sl] = rope(y).astype(BF16)

    @pl.when(jnp.logical_not(is_lat))
    def _():
        for hd in range(N_KV):
            kf_ref[:, hd * HEAD_DIM:(hd + 1) * HEAD_DIM] = kn[hd]
        vf_ref[...] = zv

    zu = _dot(hb, w_ref[:, O_UCM:O_VCM])
    zvc = _dot(hb, w_ref[:, O_VCM:O_BCV])
    n_chunks = zu.shape[0] // CHUNK
    for c in range(CM_HEADS):
        sl = slice(c * CM_DIM, (c + 1) * CM_DIM)
        vh = _rms_rows(zvc[:, sl]).astype(BF16)
        for n in range(n_chunks):
            rows = slice(n * CHUNK, (n + 1) * CHUNK)
            mixed = _dot(ws_ref[c], vh[rows]) + bs_ref[c]
            cm_ref[rows, sl] = (zu[rows, sl] * mixed).astype(BF16)

    bg_ref[...] = _dot(hb, w_ref[:, O_BCV:O_CCV])
    zc_ref[...] = _dot(hb, w_ref[:, O_CCV:O_HCV]) * _dot(hb, w_ref[:, O_HCV:D_IN])


def _stream_specs(x_parts, n_ctx_tiles):
    _, _, lat_off = x_parts
    return [pl.BlockSpec((TM, D_MODEL), lambda i: (jnp.minimum(i, n_ctx_tiles - 1), 0)),
            pl.BlockSpec((TM, D_MODEL), lambda i: (jnp.maximum(i - n_ctx_tiles, 0) + lat_off, 0))]


def _inproj(x_parts, ffn_t, mod_prev, mod, g1, w_in, q_gain, k_gain, cos, sa, sb, w_s, b_s, dims):
    n_tok, n_ctx, tps_lat = dims["n_tok"], dims["n_ctx"], dims["dec_seq"] // TM
    n_ctx_tiles = n_ctx // TM
    has_ffn = ffn_t is not None

    def mod_row(i):
        return jnp.where(i < n_ctx_tiles, 0, 1 + jnp.maximum(i - n_ctx_tiles, 0) // tps_lat)

    def pos_blk(i):
        return jnp.maximum(i - n_ctx_tiles, 0) % tps_lat

    def ctx_blk(i):
        return jnp.minimum(i, n_ctx_tiles - 1)

    tok = lambda w: pl.BlockSpec((TM, w), lambda i: (i, 0))
    full = lambda a: pl.BlockSpec(a.shape, lambda i: (0,) * a.ndim)
    mod_spec = pl.BlockSpec((1, N_MOD, D_MODEL), lambda i: (mod_row(i), 0, 0))
    rope_spec = pl.BlockSpec((TM, HEAD_DIM), lambda i: (pos_blk(i), 0))

    args, in_specs = [x_parts[0], x_parts[1]], _stream_specs(x_parts, n_ctx_tiles)
    if has_ffn:
        args += [ffn_t, mod_prev]
        in_specs += [pl.BlockSpec((D_MODEL, TM), lambda i: (0, i)), mod_spec]
    args += [mod, g1, w_in, q_gain, k_gain, cos, sa, sb, w_s, b_s]
    in_specs += [mod_spec, full(g1), full(w_in), full(q_gain), full(k_gain),
                 rope_spec, rope_spec, rope_spec, full(w_s), full(b_s)]

    out_shape, out_specs = [], []
    if has_ffn:
        out_shape.append(jax.ShapeDtypeStruct((n_tok, D_MODEL), F32))
        out_specs.append(tok(D_MODEL))
    ctx_spec = pl.BlockSpec((TM, ATT_KV_W), lambda i: (ctx_blk(i), 0))
    out_shape += [
        jax.ShapeDtypeStruct((n_tok, ATT_Q_W), BF16),
        jax.ShapeDtypeStruct((n_tok, ATT_KV_W), BF16),
        jax.ShapeDtypeStruct((n_tok, ATT_KV_W), BF16),
        jax.ShapeDtypeStruct((n_ctx, ATT_KV_W), F32),
        jax.ShapeDtypeStruct((n_ctx, ATT_KV_W), F32),
        jax.ShapeDtypeStruct((n_tok, CM_W), BF16),
        jax.ShapeDtypeStruct((n_tok, CV_W), F32),
        jax.ShapeDtypeStruct((n_tok, CV_W), F32),
    ]
    out_specs += [tok(ATT_Q_W), tok(ATT_KV_W), tok(ATT_KV_W), ctx_spec, ctx_spec,
                  tok(CM_W), tok(CV_W), tok(CV_W)]
    return pl.pallas_call(
        functools.partial(_inproj_kernel, has_ffn=has_ffn, n_ctx_tiles=n_ctx_tiles),
        grid=(n_tok // TM,),
        in_specs=in_specs,
        out_specs=out_specs,
        out_shape=out_shape,
        compiler_params=_params(("arbitrary",)),
        name="inproj",
    )(*args)


def _attend(q_ref, parts, o_ref):
    t = q_ref.shape[0]
    for kv in range(N_KV):
        ksl = slice(kv * HEAD_DIM, (kv + 1) * HEAD_DIM)
        qs = jnp.concatenate(
            [q_ref[:, (kv * GROUP + g) * HEAD_DIM:(kv * GROUP + g + 1) * HEAD_DIM] for g in range(GROUP)], axis=0)
        scores = [_dot_nt(k_ref[:, ksl], qs) for k_ref, _ in parts]
        m = scores[0].max(axis=0, keepdims=True)
        for s in scores[1:]:
            m = jnp.maximum(m, s.max(axis=0, keepdims=True))
        den = None
        num = None
        for s, (_, vt_ref) in zip(scores, parts):
            p = jnp.exp2(s - m)
            ps = p.sum(axis=0, keepdims=True)
            pv = _dot(vt_ref[ksl, :], p.astype(BF16))
            den = ps if den is None else den + ps
            num = pv if num is None else num + pv
        o = (num / den).T
        for g in range(GROUP):
            hd = kv * GROUP + g
            o_ref[:, hd * HEAD_DIM:(hd + 1) * HEAD_DIM] = o[g * t:(g + 1) * t].astype(BF16)


def _attn_ctx_kernel(q_ref, k_ref, v_ref, o_ref):
    _attend(q_ref, [(k_ref, v_ref)], o_ref)


def _attn_lat_kernel(q_ref, k_ref, v_ref, ck_ref, cv_ref, o_ref):
    _attend(q_ref, [(k_ref, v_ref), (ck_ref, cv_ref)], o_ref)


def _attention(q, kr, vbt, ck, cvt, dims):
    n_tok, n_ctx = dims["n_tok"], dims["n_ctx"]
    batch, seq, dec_batch, dec_seq = dims["batch"], dims["seq"], dims["dec_batch"], dims["dec_seq"]
    past = ck.shape[1]
    attn_ctx = pl.pallas_call(
        _attn_ctx_kernel,
        grid=(batch,),
        in_specs=[
            pl.BlockSpec((seq, ATT_Q_W), lambda b: (b, 0)),
            pl.BlockSpec((seq, ATT_KV_W), lambda b: (b, 0)),
            pl.BlockSpec((ATT_KV_W, seq), lambda b: (0, b)),
        ],
        out_specs=pl.BlockSpec((seq, ATT_Q_W), lambda b: (b, 0)),
        out_shape=jax.ShapeDtypeStruct((n_ctx, ATT_Q_W), BF16),
        compiler_params=_params(("arbitrary",)),
        name="attn_ctx",
    )(q, kr, vbt)
    nq = dec_seq // TQ
    q_off = n_ctx // TQ
    kv_off = n_ctx // dec_seq
    attn_lat = pl.pallas_call(
        _attn_lat_kernel,
        grid=(dec_batch, nq),
        in_specs=[
            pl.BlockSpec((TQ, ATT_Q_W), lambda b, j: (q_off + b * nq + j, 0)),
            pl.BlockSpec((dec_seq, ATT_KV_W), lambda b, j: (kv_off + b, 0)),
            pl.BlockSpec((ATT_KV_W, dec_seq), lambda b, j: (0, kv_off + b)),
            pl.BlockSpec((None, past, ATT_KV_W), lambda b, j: (b, 0, 0)),
            pl.BlockSpec((None, ATT_KV_W, past), lambda b, j: (b, 0, 0)),
        ],
        out_specs=pl.BlockSpec((TQ, ATT_Q_W), lambda b, j: (b * nq + j, 0)),
        out_shape=jax.ShapeDtypeStruct((n_tok - n_ctx, ATT_Q_W), BF16),
        compiler_params=_params(("arbitrary", "arbitrary")),
        name="attn_lat",
    )(q, kr, vbt, ck, cvt)
    return attn_ctx, attn_lat


def _extract_top(vals, idxs, n):
    big = jnp.int32(1 << 30)
    maxima = []
    for _ in range(n):
        m = functools.reduce(jnp.maximum, vals).max(axis=0, keepdims=True)
        cand = [jnp.where(v == m, ix, big) for v, ix in zip(vals, idxs)]
        mi = functools.reduce(jnp.minimum, cand).min(axis=0, keepdims=True)
        vals = [jnp.where(ix == mi, NEG_INF, v) for v, ix in zip(vals, idxs)]
        maxima.append(m)
    return maxima, vals


def _oddeven_merge_sort_pairs(lo, hi):
    def merge(lo, hi, r):
        step = r * 2
        if step < hi - lo:
            yield from merge(lo, hi, step)
            yield from merge(lo + r, hi, step)
            for i in range(lo + r, hi - r, step):
                yield (i, i + r)
        else:
            yield (lo, lo + r)

    if hi - lo >= 1:
        mid = lo + (hi - lo) // 2
        yield from _oddeven_merge_sort_pairs(lo, mid)
        yield from _oddeven_merge_sort_pairs(mid + 1, hi)
        yield from merge(lo, hi, 1)


_SORT16 = tuple(_oddeven_merge_sort_pairs(0, TOPK - 1))


def _compare_exchange(v, i, j):
    if v[j] is None:
        return
    if v[i] is None:
        v[i], v[j] = v[j], None
        return
    v[i], v[j] = jnp.maximum(v[i], v[j]), jnp.minimum(v[i], v[j])


def _merge_sublanes(v):
    n = len(v)
    for shift in (4, 2, 1):
        y = [None if v[n - 1 - a] is None else pltpu.roll(v[n - 1 - a], shift, 0) for a in range(n)]
        v = [y[a] if v[a] is None else (v[a] if y[a] is None else jnp.maximum(v[a], y[a])) for a in range(n)]
        for d in (8, 4, 2, 1):
            for i in range(n):
                if not i & d:
                    _compare_exchange(v, i, i + d)
    return v


def _top16_sorted(s):
    blocks = [s[SUBLANES * a:SUBLANES * (a + 1), :] for a in range(N_KEYS // SUBLANES)]
    v = list(blocks)
    for i, j in _SORT16:
        _compare_exchange(v, i, j)
    v = _merge_sublanes(v)
    count = functools.reduce(jnp.add, [jnp.where(b >= v[TOPK - 1], 1.0, 0.0) for b in blocks]).sum(axis=0, keepdims=True)
    return v, count


def _stack_rows(v, r8):
    halves = []
    for base in (0, SUBLANES):
        acc = v[base]
        for k in range(1, SUBLANES):
            acc = jnp.where(r8 == k, v[base + k], acc)
        halves.append(acc)
    return jnp.concatenate(halves, axis=0)


def _kth_largest(vals):
    chain = [vals[0]] + list(vals[2:9])
    for extra in (vals[1], vals[9]):
        x = extra
        for k in range(len(chain)):
            chain[k], x = jnp.maximum(chain[k], x), jnp.minimum(chain[k], x)
        chain.append(x)
    v = _merge_sublanes(chain + [None] * (TOPK - len(chain)))
    return v[TOPK - 1][0:1]


def _pair_blocks(m1, m2, r8):
    vals = [m1[0:1] + m2[0:8], m1[0:1] + m2[8:16], m1[1:2] + m2[0:8]]
    idxs = [r8, r8 + 8, r8 + TOPK]
    for i in range(2, 8):
        vals.append(jnp.where(r8 < TOPK // (i + 1), m1[i:i + 1] + m2[0:8], NEG_INF))
        idxs.append(r8 + TOPK * i)
    vals.append(m1[8:16] + m2[0:1])
    idxs.append((r8 + 8) * TOPK)
    return vals, idxs


def _outproj_kernel(attn_c_ref, attn_l_ref, cm_ref, bg_ref, zc_ref, zp_ref, zn_ref, xa_ref, xb_ref, mod_ref, wo_ref, cw_ref, g2_ref,
                    wk_ref,
                    x1_ref, h2_ref, b1_ref, b2_ref, tau_ref,
                    zpad_scr, st_scr, m_scr, *, n_ctx_tiles, tps_ctx, tps_lat):
    i = pl.program_id(0)
    tm = xa_ref.shape[0]
    is_lat = i >= n_ctx_tiles
    jj = jnp.where(is_lat, jnp.maximum(i - n_ctx_tiles, 0) % tps_lat, i % tps_ctx)
    nt = jnp.where(is_lat, tps_lat, tps_ctx)

    zc = zc_ref[...]
    zpad_scr[0:SUBLANES, :] = jnp.where(jj == 0, 0.0, zp_ref[...])
    zpad_scr[SUBLANES:SUBLANES + tm, :] = zc
    zpad_scr[SUBLANES + tm:2 * SUBLANES + tm, :] = jnp.where(jj == nt - 1, 0.0, zn_ref[...])
    zprev = zpad_scr[SUBLANES - 1:SUBLANES - 1 + tm, :]
    znext = zpad_scr[SUBLANES + 1:SUBLANES + 1 + tm, :]
    y = zprev * cw_ref[0:1, :] + zc * cw_ref[1:2, :] + znext * cw_ref[2:3, :]
    cv = (bg_ref[...] * y).astype(BF16)

    attn = jnp.where(is_lat, attn_l_ref[...], attn_c_ref[...])
    mix = _dot(jnp.concatenate([attn, cm_ref[...], cv], axis=1), wo_ref[...])
    g1 = mod_ref[0, 2:3, :]
    sh2 = mod_ref[0, 3:4, :]
    sc2 = mod_ref[0, 4:5, :]
    x1 = jnp.where(is_lat, xb_ref[...], xa_ref[...]) + g1 * mix
    x1_ref[...] = x1
    h2 = (_rms_rows(x1) * g2_ref[...]) * (1.0 + sc2) + sh2
    hb = h2.astype(BF16)
    h2_ref[...] = hb

    scores = _dot(hb, wk_ref[...])
    for g in range(N_GROUPS):
        st_scr[g] = scores[:, g * N_KEYS:(g + 1) * N_KEYS].T

    n_lt = tm // LANES
    r8 = lax.broadcasted_iota(jnp.int32, (SUBLANES, LANES), 0)

    def finish(h, lt, lanes, a1, a2, top, picked, vals, m1, m2):
        z = functools.reduce(
            jnp.add, [jnp.where(p, jnp.exp(v - top), 0.0) for p, v in zip(picked, vals)]).sum(axis=0, keepdims=True)
        lz = jnp.log(z)
        shifted = _pair_blocks(((m1 - top) - lz) * LOG2E - 1.0, m2 * LOG2E, r8)[0]
        tau = functools.reduce(
            jnp.minimum, [jnp.where(p, v, jnp.inf) for p, v in zip(picked, shifted)]).min(axis=0, keepdims=True)
        tau_ref[h, :, lanes] = tau
        b1_ref[h, lt] = ((a1 - top) - lz) * LOG2E - 1.0
        b2_ref[h, :, lanes] = a2 * LOG2E

    def fast(h, bad):
        for lt in range(n_lt):
            lanes = slice(lt * LANES, (lt + 1) * LANES)
            s1 = st_scr[2 * h, :, lanes]
            s2 = st_scr[2 * h + 1, :, lanes]
            v1, n1 = _top16_sorted(s1)
            v2, n2 = _top16_sorted(s2)
            m1 = _stack_rows(v1, r8)
            m2 = _stack_rows(v2, r8)
            top = m1[0:1] + m2[0:1]
            vals = _pair_blocks(m1, m2, r8)[0]
            thr = _kth_largest(vals)
            picked = [v >= thr for v in vals]
            n12 = functools.reduce(jnp.add, [jnp.where(p, 1.0, 0.0) for p in picked]).sum(axis=0, keepdims=True)
            bad = jnp.maximum(bad, jnp.where((n1 == TOPK) & (n2 == TOPK) & (n12 == TOPK), 0.0, 1.0))
            a1 = jnp.where(s1 >= v1[TOPK - 1][0:1], s1, NEG_INF)
            a2 = jnp.where(s2 >= v2[TOPK - 1][0:1], s2, NEG_INF)
            finish(h, lt, lanes, a1, a2, top, picked, vals, m1, m2)
        return bad

    bad = lax.fori_loop(0, PEER_HEADS, fast, jnp.zeros((1, LANES), F32))

    @pl.when(jnp.max(bad) > 0.0)
    def _():
        def stage1(g, carry):
            for lt in range(n_lt):
                lanes = slice(lt * LANES, (lt + 1) * LANES)
                s = st_scr[g, :, lanes]
                rows = lax.broadcasted_iota(jnp.int32, s.shape, 0)
                maxima, left = _extract_top([s], [rows], TOPK)
                m_scr[g, :, lanes] = jnp.concatenate(maxima, axis=0)
                st_scr[g, :, lanes] = jnp.where(left[0] == s, NEG_INF, s)
            return carry

        lax.fori_loop(0, N_GROUPS, stage1, 0)

        def stage2(h, carry):
            for lt in range(n_lt):
                lanes = slice(lt * LANES, (lt + 1) * LANES)
                m1 = m_scr[2 * h, :, lanes]
                m2 = m_scr[2 * h + 1, :, lanes]
                top = m1[0:1] + m2[0:1]
                vals, idxs = _pair_blocks(m1, m2, r8)
                _, left = _extract_top(vals, idxs, TOPK)
                picked = [l != v for l, v in zip(left, vals)]
                finish(h, lt, lanes, st_scr[2 * h, :, lanes], st_scr[2 * h + 1, :, lanes], top, picked, vals, m1, m2)
            return carry

        lax.fori_loop(0, PEER_HEADS, stage2, 0)


def _outproj(attn_ctx, attn_lat, cm, bg, zc, x_parts, mod, w_out, conv_w, g2, w_qk, dims):
    n_tok, n_ctx = dims["n_tok"], dims["n_ctx"]
    n_ctx_tiles = n_ctx // TM
    n_tiles = n_tok // TM
    tps_lat = dims["dec_seq"] // TM
    tps_ctx = dims["seq"] // TM
    halo_per_tile = TM // SUBLANES
    n_halo = n_tok // SUBLANES

    def mod_row(i):
        return jnp.where(i < n_ctx_tiles, 0, 1 + jnp.maximum(i - n_ctx_tiles, 0) // tps_lat)

    tok = lambda w: pl.BlockSpec((TM, w), lambda i: (i, 0))
    full = lambda a: pl.BlockSpec(a.shape, lambda i: (0,) * a.ndim)
    sel_spec = pl.BlockSpec((PEER_HEADS, N_KEYS, TM), lambda i: (0, 0, i))
    return pl.pallas_call(
        functools.partial(_outproj_kernel, n_ctx_tiles=n_ctx_tiles, tps_ctx=tps_ctx, tps_lat=tps_lat),
        grid=(n_tiles,),
        in_specs=[
            pl.BlockSpec((TM, ATT_Q_W), lambda i: (jnp.minimum(i, n_ctx_tiles - 1), 0)),
            pl.BlockSpec((TM, ATT_Q_W), lambda i: (jnp.maximum(i - n_ctx_tiles, 0), 0)),
            tok(CM_W), tok(CV_W), tok(CV_W),
            pl.BlockSpec((SUBLANES, CV_W), lambda i: (jnp.maximum(i * halo_per_tile - 1, 0), 0)),
            pl.BlockSpec((SUBLANES, CV_W), lambda i: (jnp.minimum((i + 1) * halo_per_tile, n_halo - 1), 0)),
            *_stream_specs(x_parts, n_ctx_tiles),
            pl.BlockSpec((1, N_MOD, D_MODEL), lambda i: (mod_row(i), 0, 0)),
            full(w_out), full(conv_w), full(g2), full(w_qk),
        ],
        out_specs=[
            tok(D_MODEL), tok(D_MODEL),
            pl.BlockSpec((PEER_HEADS, TM // LANES, N_KEYS, LANES), lambda i: (0, i, 0, 0)),
            sel_spec,
            pl.BlockSpec((PEER_HEADS, 1, TM), lambda i: (0, 0, i)),
        ],
        out_shape=[
            jax.ShapeDtypeStruct((n_tok, D_MODEL), F32),
            jax.ShapeDtypeStruct((n_tok, D_MODEL), BF16),
            jax.ShapeDtypeStruct((PEER_HEADS, n_tok // LANES, N_KEYS, LANES), F32),
            jax.ShapeDtypeStruct((PEER_HEADS, N_KEYS, n_tok), F32),
            jax.ShapeDtypeStruct((PEER_HEADS, 1, n_tok), F32),
        ],
        scratch_shapes=[
            pltpu.VMEM((TM + 2 * SUBLANES, CV_W), F32),
            pltpu.VMEM((N_GROUPS, N_KEYS, TM), F32),
            pltpu.VMEM((N_GROUPS, TOPK, TM), F32),
        ],
        compiler_params=_params(("arbitrary",)),
        name="outproj",
    )(attn_ctx, attn_lat, cm, bg, zc, zc, zc, x_parts[0], x_parts[1], mod, w_out, conv_w, g2, w_qk)


GELU_C0 = math.sqrt(2.0 / math.pi)
GELU_C1 = GELU_C0 * 0.044715


def _gated_gelu(half_gate, x):
    hx = half_gate * x
    return hx * jnp.tanh(x * (GELU_C0 + GELU_C1 * (x * x))) + hx


def _peer_kernel(h2_ref, b1_ref, b2_ref, tau_ref, u_ref, vt_ref, o_ref, *scratch):
    a_scrs, w_scrs = scratch[:PEER_SUB], scratch[PEER_SUB:]
    c = pl.program_id(1)
    tmp = h2_ref.shape[0]
    ec = u_ref.shape[0]
    keys_per_step = ec // N_KEYS

    n_sub = len(a_scrs)
    es = ec // n_sub
    keys_per_sub = es // N_KEYS

    def scores(s):
        a_scrs[s][...] = _dot_nt(u_ref[s * es:(s + 1) * es, :], h2_ref[...])

    def mix(s):
        o_ref[...] += _dot(vt_ref[:, s * es:(s + 1) * es], w_scrs[s][...])

    def gates(s):
        for j in range(keys_per_sub):
            i1 = c * keys_per_step + s * keys_per_sub + j
            rows = slice(j * N_KEYS, (j + 1) * N_KEYS)
            for tb in range(tmp // LANES):
                tl = slice(tb * LANES, (tb + 1) * LANES)
                gate = None
                for h in range(PEER_HEADS):
                    row = b1_ref[h, tb, pl.ds(i1, SUBLANES, stride=0), :]
                    row = jnp.broadcast_to(row[None], (N_KEYS // SUBLANES, SUBLANES, LANES)).reshape(N_KEYS, LANES)
                    t = row + b2_ref[h, :, tl]
                    g = jnp.where(t >= tau_ref[h, :, tl], jnp.exp2(t), 0.0)
                    gate = g if gate is None else gate + g
                w_scrs[s][rows, tl] = _gated_gelu(gate, a_scrs[s][rows, tl]).astype(BF16)

    @pl.when(c == 0)
    def _():
        o_ref[...] = jnp.zeros_like(o_ref)

    scores(0)
    for s in range(n_sub):
        if s + 1 < n_sub:
            scores(s + 1)
        gates(s)
        mix(s)


def _peer(h2, b1, b2, tau, u_bf, vt_bf, dims):
    n_tok = dims["n_tok"]
    n_chunks = N_EXPERTS // EC
    return pl.pallas_call(
        _peer_kernel,
        grid=(n_tok // TMP, n_chunks),
        in_specs=[
            pl.BlockSpec((TMP, D_MODEL), lambda t, c: (t, 0), pipeline_mode=pl.Buffered(1)),
            pl.BlockSpec((PEER_HEADS, TMP // LANES, N_KEYS, LANES), lambda t, c: (0, t, 0, 0),
                         pipeline_mode=pl.Buffered(1)),
            pl.BlockSpec((PEER_HEADS, N_KEYS, TMP), lambda t, c: (0, 0, t), pipeline_mode=pl.Buffered(1)),
            pl.BlockSpec((PEER_HEADS, 1, TMP), lambda t, c: (0, 0, t), pipeline_mode=pl.Buffered(1)),
            pl.BlockSpec((EC, D_MODEL), lambda t, c: (c, 0)),
            pl.BlockSpec((None, D_MODEL, EC), lambda t, c: (c, 0, 0)),
        ],
        out_specs=pl.BlockSpec((D_MODEL, TMP), lambda t, c: (0, t)),
        out_shape=jax.ShapeDtypeStruct((D_MODEL, n_tok), F32),
        scratch_shapes=([pltpu.VMEM((EC // PEER_SUB, TMP), F32)] * PEER_SUB
                        + [pltpu.VMEM((EC // PEER_SUB, TMP), BF16)] * PEER_SUB),
        compiler_params=_params(("arbitrary", "arbitrary")),
        name="peer",
    )(h2, b1, b2, tau, u_bf, vt_bf)


def _final_kernel(x_ref, ft_ref, mod_ref, g_ref, o_ref):
    x = x_ref[...] + mod_ref[0, 5:6, :] * ft_ref[...].T
    o_ref[...] = _rms_rows(x) * g_ref[...]


def _final(x1, ffn_t, mod, g_final, row0, n_rows, mod_row):
    off = row0 // TM
    return pl.pallas_call(
        _final_kernel,
        grid=(n_rows // TM,),
        in_specs=[
            pl.BlockSpec((TM, D_MODEL), lambda i: (off + i, 0)),
            pl.BlockSpec((D_MODEL, TM), lambda i: (0, off + i)),
            pl.BlockSpec((1, N_MOD, D_MODEL), lambda i: (mod_row(i), 0, 0)),
            pl.BlockSpec((1, D_MODEL), lambda i: (0, 0)),
        ],
        out_specs=pl.BlockSpec((TM, D_MODEL), lambda i: (i, 0)),
        out_shape=jax.ShapeDtypeStruct((n_rows, D_MODEL), F32),
        compiler_params=_params(("arbitrary",)),
        name="final",
    )(x1, ffn_t, mod, g_final)


def _rope_tables(n_tokens):
    rows = n_tokens // GRID_W
    t = jnp.arange(rows * GRID_W)
    pos = jnp.stack([t // GRID_W, t % GRID_W], axis=-1).astype(F32)
    n_freq = HEAD_DIM // 4
    inv = ROPE_THETA ** (-jnp.arange(n_freq, dtype=F32) / n_freq)
    ang = pos[:, :, None] * inv
    ang = jnp.broadcast_to(ang[:, :, None, :], (ang.shape[0], 2, 2, n_freq)).reshape(-1, HEAD_DIM)
    cos, sin = jnp.cos(ang), jnp.sin(ang)
    first = (jnp.arange(HEAD_DIM) % (2 * n_freq)) < n_freq
    return cos, jnp.where(first, -sin, 0.0), jnp.where(first, 0.0, sin)


def kernel(x_prompt, x_sample, cache_k, cache_v, c, c_ctx, w_ada, b_ada, g_norm1, g_norm2, w_in, q_gain, k_gain,
           w_s, b_s, conv_w, w_out, w_pq, sub_keys, u_tab, v_tab, g_final):
    batch, seq, _ = x_prompt.shape
    dec_batch, dec_seq, _ = x_sample.shape
    depth = w_in.shape[0]
    past = cache_k.shape[2]
    n_ctx = batch * seq
    n_tok = n_ctx + dec_batch * dec_seq
    dims = dict(batch=batch, seq=seq, dec_batch=dec_batch, dec_seq=dec_seq, n_ctx=n_ctx, n_tok=n_tok)
    assert seq % TM == 0 and dec_seq % TM == 0 and n_ctx % dec_seq == 0 and n_tok % TMP == 0
    assert 1 + dec_batch <= COND_ROWS

    x_parts = (x_prompt.reshape(n_ctx, D_MODEL), x_sample.reshape(dec_batch * dec_seq, D_MODEL), 0)
    cond = jnp.zeros((COND_ROWS, D_MODEL), F32).at[0].set(c_ctx).at[1:1 + dec_batch].set(c)
    mod_all = _ada(cond, w_ada, b_ada).reshape(depth, COND_ROWS, N_MOD, D_MODEL)
    cos, sa, sb = _rope_tables(dec_seq)
    w_qk_all = _query_key_weights(w_pq, sub_keys)

    tps_lat = dec_seq // TM
    n_ctx_tiles = n_ctx // TM
    ffn_t = None
    mod_prev = None
    new_k, new_v = [], []
    for l in range(depth):
        mod = mod_all[l]
        w_in_bf = w_in[l].astype(BF16)
        w_out_bf = w_out[l].astype(BF16)
        u_bf = u_tab[l].astype(BF16)
        vt_bf = v_tab[l].astype(BF16).reshape(N_EXPERTS // EC, EC, D_MODEL).transpose(0, 2, 1)
        ws_bf = w_s[l].astype(BF16)
        bs_b = jnp.broadcast_to(b_s[l][:, :, None], (CM_HEADS, CHUNK, CM_DIM))
        ck = cache_k[:, l].reshape(dec_batch, past, ATT_KV_W).astype(BF16)
        cvt = cache_v[:, l].reshape(dec_batch, past, ATT_KV_W).astype(BF16).transpose(0, 2, 1)

        outs = _inproj(x_parts, ffn_t, mod_prev, mod, g_norm1[l][None], w_in_bf, q_gain[l][None], k_gain[l][None],
                       cos, sa, sb, ws_bf, bs_b, dims)
        if ffn_t is not None:
            x_parts = (outs[0], outs[0], n_ctx_tiles)
            outs = outs[1:]
        q, kr, vb, kf, vf, cm, bg, zc = outs
        new_k.append(kf.reshape(batch, seq, N_KV, HEAD_DIM))
        new_v.append(vf.reshape(batch, seq, N_KV, HEAD_DIM))
        attn_ctx, attn_lat = _attention(q, kr, vb.T, ck, cvt, dims)
        x, h2, b1, b2, tau = _outproj(attn_ctx, attn_lat, cm, bg, zc, x_parts, mod, w_out_bf, conv_w[l],
                                      g_norm2[l][None], w_qk_all[l], dims)
        x_parts = (x, x, n_ctx_tiles)
        ffn_t = _peer(h2, b1, b2, tau, u_bf, vt_bf, dims)
        mod_prev = mod

    gf = g_final[None]
    y_prompt = _final(x, ffn_t, mod_prev, gf, 0, n_ctx, lambda i: 0)
    y_sample = _final(x, ffn_t, mod_prev, gf, n_ctx, n_tok - n_ctx, lambda i: 1 + i // tps_lat)
    return (y_prompt.reshape(batch, seq, D_MODEL), y_sample.reshape(dec_batch, dec_seq, D_MODEL),
            jnp.stack(new_k, axis=1), jnp.stack(new_v, axis=1))
```

```python
import functools
import math

import jax
import jax.numpy as jnp
from jax import lax
from jax.experimental import pallas as pl
from jax.experimental.pallas import tpu as pltpu

F32 = jnp.float32
BF16 = jnp.bfloat16

D_MODEL = 2048
GRID_W = 64
N_HEADS = 8
N_KV = 2
HEAD_DIM = 128
GROUP = N_HEADS // N_KV
ATT_Q_W = N_HEADS * HEAD_DIM
ATT_KV_W = N_KV * HEAD_DIM
ROPE_THETA = 10000.0
CHUNK = 128
CM_HEADS = 4
CM_DIM = 128
CM_W = CM_HEADS * CM_DIM
CV_W = 512
D_IN = ATT_Q_W + 2 * ATT_KV_W + 2 * CM_W + 3 * CV_W
PEER_HEADS = 8
N_KEYS = 128
N_EXPERTS = N_KEYS * N_KEYS
TOPK = 16
PK_HALF = 128
N_GROUPS = 2 * PEER_HEADS
N_MOD = 6
EPS = 1e-6
NEG_INF = float("-inf")
LOG2E = math.log2(math.e)
Q_SCALE = HEAD_DIM ** -0.5 * LOG2E

LANES = 128
SUBLANES = 8
VMEM_LIMIT = 56 * 1024 * 1024

TM = 256
TQ = 256
TMP = 1024
EC = 1024
PEER_SUB = 2
ADA_NT = 1024
COND_ROWS = 16

O_Q = 0
O_K = O_Q + ATT_Q_W
O_V = O_K + ATT_KV_W
O_UCM = O_V + ATT_KV_W
O_VCM = O_UCM + CM_W
O_BCV = O_VCM + CM_W
O_CCV = O_BCV + CV_W
O_HCV = O_CCV + CV_W


def _dot(a, b):
    return jnp.dot(a, b, preferred_element_type=F32)


def _dot_nt(a, b):
    return lax.dot_general(a, b, (((1,), (1,)), ((), ())), preferred_element_type=F32)


def _split(x):
    hi = x.astype(BF16)
    lo = (x - hi.astype(F32)).astype(BF16)
    return hi, lo


def _rms_rows(x):
    return x * lax.rsqrt(jnp.mean(x * x, axis=-1, keepdims=True) + EPS)


def _params(sem):
    return pltpu.CompilerParams(dimension_semantics=sem, vmem_limit_bytes=VMEM_LIMIT)


def _ada_kernel(c_ref, w_ref, b_ref, o_ref):
    c = c_ref[...]
    a = c * jax.nn.sigmoid(c)
    a_hi, a_lo = _split(a)
    w_hi, w_lo = _split(w_ref[0])
    o_ref[0] = _dot(a_hi, w_hi) + _dot(a_lo, w_hi) + _dot(a_hi, w_lo) + b_ref[0]


def _ada(cond, w_ada, b_ada):
    depth, _, n_out = w_ada.shape
    return pl.pallas_call(
        _ada_kernel,
        grid=(depth, n_out // ADA_NT),
        in_specs=[
            pl.BlockSpec((COND_ROWS, D_MODEL), lambda l, j: (0, 0)),
            pl.BlockSpec((1, D_MODEL, ADA_NT), lambda l, j: (l, 0, j)),
            pl.BlockSpec((1, 1, ADA_NT), lambda l, j: (l, 0, j)),
        ],
        out_specs=pl.BlockSpec((1, COND_ROWS, ADA_NT), lambda l, j: (l, 0, j)),
        out_shape=jax.ShapeDtypeStruct((depth, COND_ROWS, n_out), F32),
        compiler_params=_params(("arbitrary", "arbitrary")),
        name="ada",
    )(cond, w_ada, b_ada.reshape(depth, 1, n_out))


def _qk_kernel(w_ref, k_ref, o_ref):
    w_hi, w_lo = _split(w_ref[0])
    k_hi, k_lo = _split(k_ref[0, 0])
    o_ref[0] = (_dot_nt(w_hi, k_hi) + _dot_nt(w_lo, k_hi) + _dot_nt(w_hi, k_lo)).astype(BF16)


def _query_key_weights(w_pq, sub_keys):
    depth = w_pq.shape[0]
    return pl.pallas_call(
        _qk_kernel,
        grid=(depth, N_GROUPS),
        in_specs=[
            pl.BlockSpec((1, D_MODEL, PK_HALF), lambda l, g: (l, 0, g)),
            pl.BlockSpec((1, 1, N_KEYS, PK_HALF), lambda l, g: (l, g, 0, 0)),
        ],
        out_specs=pl.BlockSpec((1, D_MODEL, N_KEYS), lambda l, g: (l, 0, g)),
        out_shape=jax.ShapeDtypeStruct((depth, D_MODEL, N_GROUPS * N_KEYS), BF16),
        compiler_params=_params(("arbitrary", "arbitrary")),
        name="qk_weights",
    )(w_pq, sub_keys.reshape(depth, N_GROUPS, N_KEYS, PK_HALF))


def _value_blocks_kernel(v_ref, o_ref):
    o_ref[...] = v_ref[...].T.astype(BF16)


def _value_blocks(v_tab):
    depth = v_tab.shape[0]
    n_chunks = N_EXPERTS // EC
    return pl.pallas_call(
        _value_blocks_kernel,
        grid=(depth, n_chunks),
        in_specs=[pl.BlockSpec((None, EC, D_MODEL), lambda l, c: (l, c, 0))],
        out_specs=pl.BlockSpec((None, None, D_MODEL, EC), lambda l, c: (l, c, 0, 0)),
        out_shape=jax.ShapeDtypeStruct((depth, n_chunks, D_MODEL, EC), BF16),
        compiler_params=_params(("arbitrary", "arbitrary")),
        name="value_blocks",
    )(v_tab)


def _inproj_kernel(*refs, has_ffn, n_ctx_tiles):
    refs = list(refs)
    xa_ref = refs.pop(0)
    xb_ref = refs.pop(0)
    if has_ffn:
        ft_ref = refs.pop(0)
        modp_ref = refs.pop(0)
    (mod_ref, g1_ref, w_ref, qg_ref, kg_ref, cos_ref, sa_ref, sb_ref, ws_ref, bs_ref) = refs[:10]
    outs = refs[10:]
    if has_ffn:
        xo_ref = outs.pop(0)
    q_ref, kr_ref, vb_ref, kf_ref, vf_ref, cm_ref, bg_ref, zc_ref = outs

    i = pl.program_id(0)
    is_lat = i >= n_ctx_tiles
    x = jnp.where(is_lat, xb_ref[...], xa_ref[...])
    if has_ffn:
        x = x + modp_ref[0, 5:6, :] * ft_ref[...].T
        xo_ref[...] = x
    sh1 = mod_ref[0, 0:1, :]
    sc1 = mod_ref[0, 1:2, :]
    h = _rms_rows(x) * g1_ref[...]
    hb = (h * (1.0 + sc1) + sh1).astype(BF16)

    cos = jnp.where(is_lat, cos_ref[...], 1.0)
    sa = jnp.where(is_lat, sa_ref[...], 0.0)
    sb = jnp.where(is_lat, sb_ref[...], 0.0)

    def rope(y):
        return y * cos + pltpu.roll(y, 96, 1) * sa + pltpu.roll(y, 32, 1) * sb

    zq = _dot(hb, w_ref[:, O_Q:O_K])
    for hd in range(N_HEADS):
        sl = slice(hd * HEAD_DIM, (hd + 1) * HEAD_DIM)
        y = _rms_rows(zq[:, sl]) * qg_ref[...]
        q_ref[:, sl] = (rope(y) * Q_SCALE).astype(BF16)

    zk = _dot(hb, w_ref[:, O_K:O_V])
    zv = _dot(hb, w_ref[:, O_V:O_UCM])
    vb_ref[...] = zv.astype(BF16)
    kn = []
    for hd in range(N_KV):
        sl = slice(hd * HEAD_DIM, (hd + 1) * HEAD_DIM)
        y = _rms_rows(zk[:, sl]) * kg_ref[...]
        kn.append(y)
        kr_ref[:, sl] = rope(y).astype(BF16)

    @pl.when(jnp.logical_not(is_lat))
    def _():
        for hd in range(N_KV):
            kf_ref[:, hd * HEAD_DIM:(hd + 1) * HEAD_DIM] = kn[hd]
        vf_ref[...] = zv

    zu = _dot(hb, w_ref[:, O_UCM:O_VCM])
    zvc = _dot(hb, w_ref[:, O_VCM:O_BCV])
    n_chunks = zu.shape[0] // CHUNK
    for c in range(CM_HEADS):
        sl = slice(c * CM_DIM, (c + 1) * CM_DIM)
        vh = _rms_rows(zvc[:, sl]).astype(BF16)
        for n in range(n_chunks):
            rows = slice(n * CHUNK, (n + 1) * CHUNK)
            mixed = _dot(ws_ref[c], vh[rows]) + bs_ref[c]
            cm_ref[rows, sl] = (zu[rows, sl] * mixed).astype(BF16)

    bg_ref[...] = _dot(hb, w_ref[:, O_BCV:O_CCV])
    zc_ref[...] = _dot(hb, w_ref[:, O_CCV:O_HCV]) * _dot(hb, w_ref[:, O_HCV:D_IN])


def _layer_spec(stacked, layer):
    return pl.BlockSpec((None,) + stacked.shape[1:], lambda i: (layer,) + (0,) * (stacked.ndim - 1),
                        pipeline_mode=pl.Buffered(1))


def _stream_specs(x_parts, n_ctx_tiles):
    _, _, lat_off = x_parts
    return [pl.BlockSpec((TM, D_MODEL), lambda i: (jnp.minimum(i, n_ctx_tiles - 1), 0)),
            pl.BlockSpec((TM, D_MODEL), lambda i: (jnp.maximum(i - n_ctx_tiles, 0) + lat_off, 0))]


def _inproj(x_parts, ffn_t, mod_prev, mod, g1, w_in, q_gain, k_gain, cos, sa, sb, w_s, b_s, dims):
    n_tok, n_ctx, tps_lat = dims["n_tok"], dims["n_ctx"], dims["dec_seq"] // TM
    n_ctx_tiles = n_ctx // TM
    has_ffn = ffn_t is not None

    def mod_row(i):
        return jnp.where(i < n_ctx_tiles, 0, 1 + jnp.maximum(i - n_ctx_tiles, 0) // tps_lat)

    def pos_blk(i):
        return jnp.maximum(i - n_ctx_tiles, 0) % tps_lat

    def ctx_blk(i):
        return jnp.minimum(i, n_ctx_tiles - 1)

    tok = lambda w: pl.BlockSpec((TM, w), lambda i: (i, 0))
    full = lambda a: pl.BlockSpec(a.shape, lambda i: (0,) * a.ndim)
    mod_spec = pl.BlockSpec((1, N_MOD, D_MODEL), lambda i: (mod_row(i), 0, 0))
    rope_spec = pl.BlockSpec((TM, HEAD_DIM), lambda i: (pos_blk(i), 0))

    args, in_specs = [x_parts[0], x_parts[1]], _stream_specs(x_parts, n_ctx_tiles)
    if has_ffn:
        args += [ffn_t, mod_prev]
        in_specs += [pl.BlockSpec((D_MODEL, TM), lambda i: (0, i)), mod_spec]
    args += [mod, g1, w_in[0], q_gain, k_gain, cos, sa, sb, w_s, b_s]
    in_specs += [mod_spec, full(g1), _layer_spec(*w_in), full(q_gain), full(k_gain),
                 rope_spec, rope_spec, rope_spec, full(w_s), full(b_s)]

    out_shape, out_specs = [], []
    if has_ffn:
        out_shape.append(jax.ShapeDtypeStruct((n_tok, D_MODEL), F32))
        out_specs.append(tok(D_MODEL))
    ctx_spec = pl.BlockSpec((TM, ATT_KV_W), lambda i: (ctx_blk(i), 0))
    out_shape += [
        jax.ShapeDtypeStruct((n_tok, ATT_Q_W), BF16),
        jax.ShapeDtypeStruct((n_tok, ATT_KV_W), BF16),
        jax.ShapeDtypeStruct((n_tok, ATT_KV_W), BF16),
        jax.ShapeDtypeStruct((n_ctx, ATT_KV_W), F32),
        jax.ShapeDtypeStruct((n_ctx, ATT_KV_W), F32),
        jax.ShapeDtypeStruct((n_tok, CM_W), BF16),
        jax.ShapeDtypeStruct((n_tok, CV_W), F32),
        jax.ShapeDtypeStruct((n_tok, CV_W), F32),
    ]
    out_specs += [tok(ATT_Q_W), tok(ATT_KV_W), tok(ATT_KV_W), ctx_spec, ctx_spec,
                  tok(CM_W), tok(CV_W), tok(CV_W)]
    return pl.pallas_call(
        functools.partial(_inproj_kernel, has_ffn=has_ffn, n_ctx_tiles=n_ctx_tiles),
        grid=(n_tok // TM,),
        in_specs=in_specs,
        out_specs=out_specs,
        out_shape=out_shape,
        compiler_params=_params(("arbitrary",)),
        name="inproj",
    )(*args)


def _attend(q_ref, parts, o_ref):
    t = q_ref.shape[0]
    for kv in range(N_KV):
        ksl = slice(kv * HEAD_DIM, (kv + 1) * HEAD_DIM)
        qs = jnp.concatenate(
            [q_ref[:, (kv * GROUP + g) * HEAD_DIM:(kv * GROUP + g + 1) * HEAD_DIM] for g in range(GROUP)], axis=0)
        scores = [_dot_nt(k_ref[:, ksl], qs) for k_ref, _ in parts]
        m = scores[0].max(axis=0, keepdims=True)
        for s in scores[1:]:
            m = jnp.maximum(m, s.max(axis=0, keepdims=True))
        den = None
        num = None
        for s, (_, vt_ref) in zip(scores, parts):
            p = jnp.exp2(s - m)
            ps = p.sum(axis=0, keepdims=True)
            pv = _dot(vt_ref[ksl, :], p.astype(BF16))
            den = ps if den is None else den + ps
            num = pv if num is None else num + pv
        o = (num / den).T
        for g in range(GROUP):
            hd = kv * GROUP + g
            o_ref[:, hd * HEAD_DIM:(hd + 1) * HEAD_DIM] = o[g * t:(g + 1) * t].astype(BF16)


def _attn_ctx_kernel(q_ref, k_ref, v_ref, o_ref):
    _attend(q_ref, [(k_ref, v_ref)], o_ref)


def _attn_lat_kernel(q_ref, k_ref, v_ref, ck_ref, cv_ref, o_ref):
    _attend(q_ref, [(k_ref, v_ref), (ck_ref, cv_ref)], o_ref)


def _attention(q, kr, vbt, ck, cvt, dims):
    n_tok, n_ctx = dims["n_tok"], dims["n_ctx"]
    batch, seq, dec_batch, dec_seq = dims["batch"], dims["seq"], dims["dec_batch"], dims["dec_seq"]
    past = ck.shape[1]
    attn_ctx = pl.pallas_call(
        _attn_ctx_kernel,
        grid=(batch,),
        in_specs=[
            pl.BlockSpec((seq, ATT_Q_W), lambda b: (b, 0)),
            pl.BlockSpec((seq, ATT_KV_W), lambda b: (b, 0)),
            pl.BlockSpec((ATT_KV_W, seq), lambda b: (0, b)),
        ],
        out_specs=pl.BlockSpec((seq, ATT_Q_W), lambda b: (b, 0)),
        out_shape=jax.ShapeDtypeStruct((n_ctx, ATT_Q_W), BF16),
        compiler_params=_params(("arbitrary",)),
        name="attn_ctx",
    )(q, kr, vbt)
    nq = dec_seq // TQ
    q_off = n_ctx // TQ
    kv_off = n_ctx // dec_seq
    attn_lat = pl.pallas_call(
        _attn_lat_kernel,
        grid=(dec_batch, nq),
        in_specs=[
            pl.BlockSpec((TQ, ATT_Q_W), lambda b, j: (q_off + b * nq + j, 0)),
            pl.BlockSpec((dec_seq, ATT_KV_W), lambda b, j: (kv_off + b, 0)),
            pl.BlockSpec((ATT_KV_W, dec_seq), lambda b, j: (0, kv_off + b)),
            pl.BlockSpec((None, past, ATT_KV_W), lambda b, j: (b, 0, 0)),
            pl.BlockSpec((None, ATT_KV_W, past), lambda b, j: (b, 0, 0)),
        ],
        out_specs=pl.BlockSpec((TQ, ATT_Q_W), lambda b, j: (b * nq + j, 0)),
        out_shape=jax.ShapeDtypeStruct((n_tok - n_ctx, ATT_Q_W), BF16),
        compiler_params=_params(("arbitrary", "arbitrary")),
        name="attn_lat",
    )(q, kr, vbt, ck, cvt)
    return attn_ctx, attn_lat


def _extract_top(vals, idxs, n):
    big = jnp.int32(1 << 30)
    maxima = []
    for _ in range(n):
        m = functools.reduce(jnp.maximum, vals).max(axis=0, keepdims=True)
        cand = [jnp.where(v == m, ix, big) for v, ix in zip(vals, idxs)]
        mi = functools.reduce(jnp.minimum, cand).min(axis=0, keepdims=True)
        vals = [jnp.where(ix == mi, NEG_INF, v) for v, ix in zip(vals, idxs)]
        maxima.append(m)
    return maxima, vals


def _oddeven_merge_sort_pairs(lo, hi):
    def merge(lo, hi, r):
        step = r * 2
        if step < hi - lo:
            yield from merge(lo, hi, step)
            yield from merge(lo + r, hi, step)
            for i in range(lo + r, hi - r, step):
                yield (i, i + r)
        else:
            yield (lo, lo + r)

    if hi - lo >= 1:
        mid = lo + (hi - lo) // 2
        yield from _oddeven_merge_sort_pairs(lo, mid)
        yield from _oddeven_merge_sort_pairs(mid + 1, hi)
        yield from merge(lo, hi, 1)


_SORT16 = tuple(_oddeven_merge_sort_pairs(0, TOPK - 1))


def _compare_exchange(v, i, j):
    if v[j] is None:
        return
    if v[i] is None:
        v[i], v[j] = v[j], None
        return
    v[i], v[j] = jnp.maximum(v[i], v[j]), jnp.minimum(v[i], v[j])


def _merge_sublanes(v):
    n = len(v)
    for shift in (4, 2, 1):
        y = [None if v[n - 1 - a] is None else pltpu.roll(v[n - 1 - a], shift, 0) for a in range(n)]
        v = [y[a] if v[a] is None else (v[a] if y[a] is None else jnp.maximum(v[a], y[a])) for a in range(n)]
        for d in (8, 4, 2, 1):
            for i in range(n):
                if not i & d:
                    _compare_exchange(v, i, i + d)
    return v


def _top16_sorted(s):
    blocks = [s[SUBLANES * a:SUBLANES * (a + 1), :] for a in range(N_KEYS // SUBLANES)]
    v = list(blocks)
    for i, j in _SORT16:
        _compare_exchange(v, i, j)
    v = _merge_sublanes(v)
    count = functools.reduce(jnp.add, [jnp.where(b >= v[TOPK - 1], 1.0, 0.0) for b in blocks]).sum(axis=0, keepdims=True)
    return v, count


def _stack_rows(v, r8):
    halves = []
    for base in (0, SUBLANES):
        acc = v[base]
        for k in range(1, SUBLANES):
            acc = jnp.where(r8 == k, v[base + k], acc)
        halves.append(acc)
    return jnp.concatenate(halves, axis=0)


def _kth_largest(vals):
    chain = [vals[0]] + list(vals[2:9])
    for extra in (vals[1], vals[9]):
        x = extra
        for k in range(len(chain)):
            chain[k], x = jnp.maximum(chain[k], x), jnp.minimum(chain[k], x)
        chain.append(x)
    v = _merge_sublanes(chain + [None] * (TOPK - len(chain)))
    return v[TOPK - 1][0:1]


def _pair_blocks(m1, m2, r8):
    vals = [m1[0:1] + m2[0:8], m1[0:1] + m2[8:16], m1[1:2] + m2[0:8]]
    idxs = [r8, r8 + 8, r8 + TOPK]
    for i in range(2, 8):
        vals.append(jnp.where(r8 < TOPK // (i + 1), m1[i:i + 1] + m2[0:8], NEG_INF))
        idxs.append(r8 + TOPK * i)
    vals.append(m1[8:16] + m2[0:1])
    idxs.append((r8 + 8) * TOPK)
    return vals, idxs


def _outproj_kernel(attn_c_ref, attn_l_ref, cm_ref, bg_ref, zc_ref, zp_ref, zn_ref, xa_ref, xb_ref, mod_ref, wo_ref, cw_ref, g2_ref,
                    wk_ref,
                    x1_ref, h2_ref, b1_ref, b2_ref, tau_ref,
                    zpad_scr, st_scr, m_scr, *, n_ctx_tiles, tps_ctx, tps_lat):
    i = pl.program_id(0)
    tm = xa_ref.shape[0]
    is_lat = i >= n_ctx_tiles
    jj = jnp.where(is_lat, jnp.maximum(i - n_ctx_tiles, 0) % tps_lat, i % tps_ctx)
    nt = jnp.where(is_lat, tps_lat, tps_ctx)

    zc = zc_ref[...]
    zpad_scr[0:SUBLANES, :] = jnp.where(jj == 0, 0.0, zp_ref[...])
    zpad_scr[SUBLANES:SUBLANES + tm, :] = zc
    zpad_scr[SUBLANES + tm:2 * SUBLANES + tm, :] = jnp.where(jj == nt - 1, 0.0, zn_ref[...])
    zprev = zpad_scr[SUBLANES - 1:SUBLANES - 1 + tm, :]
    znext = zpad_scr[SUBLANES + 1:SUBLANES + 1 + tm, :]
    y = zprev * cw_ref[0:1, :] + zc * cw_ref[1:2, :] + znext * cw_ref[2:3, :]
    cv = (bg_ref[...] * y).astype(BF16)

    attn = jnp.where(is_lat, attn_l_ref[...], attn_c_ref[...])
    mix = _dot(jnp.concatenate([attn, cm_ref[...], cv], axis=1), wo_ref[...])
    g1 = mod_ref[0, 2:3, :]
    sh2 = mod_ref[0, 3:4, :]
    sc2 = mod_ref[0, 4:5, :]
    x1 = jnp.where(is_lat, xb_ref[...], xa_ref[...]) + g1 * mix
    x1_ref[...] = x1
    h2 = (_rms_rows(x1) * g2_ref[...]) * (1.0 + sc2) + sh2
    hb = h2.astype(BF16)
    h2_ref[...] = hb

    scores = _dot(hb, wk_ref[...])
    for g in range(N_GROUPS):
        st_scr[g] = scores[:, g * N_KEYS:(g + 1) * N_KEYS].T

    n_lt = tm // LANES
    r8 = lax.broadcasted_iota(jnp.int32, (SUBLANES, LANES), 0)

    def finish(h, lt, lanes, a1, a2, top, picked, vals, m1, m2):
        z = functools.reduce(
            jnp.add, [jnp.where(p, jnp.exp(v - top), 0.0) for p, v in zip(picked, vals)]).sum(axis=0, keepdims=True)
        lz = jnp.log(z)
        shifted = _pair_blocks(((m1 - top) - lz) * LOG2E - 1.0, m2 * LOG2E, r8)[0]
        tau = functools.reduce(
            jnp.minimum, [jnp.where(p, v, jnp.inf) for p, v in zip(picked, shifted)]).min(axis=0, keepdims=True)
        tau_ref[h, :, lanes] = tau
        b1_ref[h, lt] = ((a1 - top) - lz) * LOG2E - 1.0
        b2_ref[h, :, lanes] = a2 * LOG2E

    def fast(h, bad):
        for lt in range(n_lt):
            lanes = slice(lt * LANES, (lt + 1) * LANES)
            s1 = st_scr[2 * h, :, lanes]
            s2 = st_scr[2 * h + 1, :, lanes]
            v1, n1 = _top16_sorted(s1)
            v2, n2 = _top16_sorted(s2)
            m1 = _stack_rows(v1, r8)
            m2 = _stack_rows(v2, r8)
            top = m1[0:1] + m2[0:1]
            vals = _pair_blocks(m1, m2, r8)[0]
            thr = _kth_largest(vals)
            picked = [v >= thr for v in vals]
            n12 = functools.reduce(jnp.add, [jnp.where(p, 1.0, 0.0) for p in picked]).sum(axis=0, keepdims=True)
            bad = jnp.maximum(bad, jnp.where((n1 == TOPK) & (n2 == TOPK) & (n12 == TOPK), 0.0, 1.0))
            a1 = jnp.where(s1 >= v1[TOPK - 1][0:1], s1, NEG_INF)
            a2 = jnp.where(s2 >= v2[TOPK - 1][0:1], s2, NEG_INF)
            finish(h, lt, lanes, a1, a2, top, picked, vals, m1, m2)
        return bad

    bad = lax.fori_loop(0, PEER_HEADS, fast, jnp.zeros((1, LANES), F32))

    @pl.when(jnp.max(bad) > 0.0)
    def _():
        def stage1(g, carry):
            for lt in range(n_lt):
                lanes = slice(lt * LANES, (lt + 1) * LANES)
                s = st_scr[g, :, lanes]
                rows = lax.broadcasted_iota(jnp.int32, s.shape, 0)
                maxima, left = _extract_top([s], [rows], TOPK)
                m_scr[g, :, lanes] = jnp.concatenate(maxima, axis=0)
                st_scr[g, :, lanes] = jnp.where(left[0] == s, NEG_INF, s)
            return carry

        lax.fori_loop(0, N_GROUPS, stage1, 0)

        def stage2(h, carry):
            for lt in range(n_lt):
                lanes = slice(lt * LANES, (lt + 1) * LANES)
                m1 = m_scr[2 * h, :, lanes]
                m2 = m_scr[2 * h + 1, :, lanes]
                top = m1[0:1] + m2[0:1]
                vals, idxs = _pair_blocks(m1, m2, r8)
                _, left = _extract_top(vals, idxs, TOPK)
                picked = [l != v for l, v in zip(left, vals)]
                finish(h, lt, lanes, st_scr[2 * h, :, lanes], st_scr[2 * h + 1, :, lanes], top, picked, vals, m1, m2)
            return carry

        lax.fori_loop(0, PEER_HEADS, stage2, 0)


def _outproj(attn_ctx, attn_lat, cm, bg, zc, x_parts, mod, w_out, conv_w, g2, w_qk, dims):
    n_tok, n_ctx = dims["n_tok"], dims["n_ctx"]
    n_ctx_tiles = n_ctx // TM
    n_tiles = n_tok // TM
    tps_lat = dims["dec_seq"] // TM
    tps_ctx = dims["seq"] // TM
    halo_per_tile = TM // SUBLANES
    n_halo = n_tok // SUBLANES

    def mod_row(i):
        return jnp.where(i < n_ctx_tiles, 0, 1 + jnp.maximum(i - n_ctx_tiles, 0) // tps_lat)

    tok = lambda w: pl.BlockSpec((TM, w), lambda i: (i, 0))
    full = lambda a: pl.BlockSpec(a.shape, lambda i: (0,) * a.ndim)
    sel_spec = pl.BlockSpec((PEER_HEADS, N_KEYS, TM), lambda i: (0, 0, i))
    return pl.pallas_call(
        functools.partial(_outproj_kernel, n_ctx_tiles=n_ctx_tiles, tps_ctx=tps_ctx, tps_lat=tps_lat),
        grid=(n_tiles,),
        in_specs=[
            pl.BlockSpec((TM, ATT_Q_W), lambda i: (jnp.minimum(i, n_ctx_tiles - 1), 0)),
            pl.BlockSpec((TM, ATT_Q_W), lambda i: (jnp.maximum(i - n_ctx_tiles, 0), 0)),
            tok(CM_W), tok(CV_W), tok(CV_W),
            pl.BlockSpec((SUBLANES, CV_W), lambda i: (jnp.maximum(i * halo_per_tile - 1, 0), 0)),
            pl.BlockSpec((SUBLANES, CV_W), lambda i: (jnp.minimum((i + 1) * halo_per_tile, n_halo - 1), 0)),
            *_stream_specs(x_parts, n_ctx_tiles),
            pl.BlockSpec((1, N_MOD, D_MODEL), lambda i: (mod_row(i), 0, 0)),
            _layer_spec(*w_out), full(conv_w), full(g2), _layer_spec(*w_qk),
        ],
        out_specs=[
            tok(D_MODEL), tok(D_MODEL),
            pl.BlockSpec((PEER_HEADS, TM // LANES, N_KEYS, LANES), lambda i: (0, i, 0, 0)),
            sel_spec,
            pl.BlockSpec((PEER_HEADS, 1, TM), lambda i: (0, 0, i)),
        ],
        out_shape=[
            jax.ShapeDtypeStruct((n_tok, D_MODEL), F32),
            jax.ShapeDtypeStruct((n_tok, D_MODEL), BF16),
            jax.ShapeDtypeStruct((PEER_HEADS, n_tok // LANES, N_KEYS, LANES), F32),
            jax.ShapeDtypeStruct((PEER_HEADS, N_KEYS, n_tok), F32),
            jax.ShapeDtypeStruct((PEER_HEADS, 1, n_tok), F32),
        ],
        scratch_shapes=[
            pltpu.VMEM((TM + 2 * SUBLANES, CV_W), F32),
            pltpu.VMEM((N_GROUPS, N_KEYS, TM), F32),
            pltpu.VMEM((N_GROUPS, TOPK, TM), F32),
        ],
        compiler_params=_params(("arbitrary",)),
        name="outproj",
    )(attn_ctx, attn_lat, cm, bg, zc, zc, zc, x_parts[0], x_parts[1], mod, w_out[0], conv_w, g2, w_qk[0])


GELU_C0 = math.sqrt(2.0 / math.pi)
GELU_C1 = GELU_C0 * 0.044715


def _gated_gelu(half_gate, x):
    hx = half_gate * x
    return hx * jnp.tanh(x * (GELU_C0 + GELU_C1 * (x * x))) + hx


def _peer_kernel(h2_ref, b1_ref, b2_ref, tau_ref, u_ref, vt_ref, o_ref, *scratch):
    a_scrs, w_scrs = scratch[:PEER_SUB], scratch[PEER_SUB:]
    c = pl.program_id(1)
    tmp = h2_ref.shape[0]
    ec = u_ref.shape[0]
    keys_per_step = ec // N_KEYS

    n_sub = len(a_scrs)
    es = ec // n_sub
    keys_per_sub = es // N_KEYS

    def scores(s):
        a_scrs[s][...] = _dot_nt(u_ref[s * es:(s + 1) * es, :], h2_ref[...])

    def mix(s):
        o_ref[...] += _dot(vt_ref[:, s * es:(s + 1) * es], w_scrs[s][...])

    def gates(s):
        for j in range(keys_per_sub):
            i1 = c * keys_per_step + s * keys_per_sub + j
            rows = slice(j * N_KEYS, (j + 1) * N_KEYS)
            for tb in range(tmp // LANES):
                tl = slice(tb * LANES, (tb + 1) * LANES)
                gate = None
                for h in range(PEER_HEADS):
                    row = b1_ref[h, tb, pl.ds(i1, SUBLANES, stride=0), :]
                    row = jnp.broadcast_to(row[None], (N_KEYS // SUBLANES, SUBLANES, LANES)).reshape(N_KEYS, LANES)
                    t = row + b2_ref[h, :, tl]
                    g = jnp.where(t >= tau_ref[h, :, tl], jnp.exp2(t), 0.0)
                    gate = g if gate is None else gate + g
                w_scrs[s][rows, tl] = _gated_gelu(gate, a_scrs[s][rows, tl]).astype(BF16)

    @pl.when(c == 0)
    def _():
        o_ref[...] = jnp.zeros_like(o_ref)

    scores(0)
    for s in range(n_sub):
        if s + 1 < n_sub:
            scores(s + 1)
        gates(s)
        mix(s)


def _peer(h2, b1, b2, tau, u_all, vt_all, layer, dims):
    n_tok = dims["n_tok"]
    n_chunks = N_EXPERTS // EC
    return pl.pallas_call(
        _peer_kernel,
        grid=(n_tok // TMP, n_chunks),
        in_specs=[
            pl.BlockSpec((TMP, D_MODEL), lambda t, c: (t, 0), pipeline_mode=pl.Buffered(1)),
            pl.BlockSpec((PEER_HEADS, TMP // LANES, N_KEYS, LANES), lambda t, c: (0, t, 0, 0),
                         pipeline_mode=pl.Buffered(1)),
            pl.BlockSpec((PEER_HEADS, N_KEYS, TMP), lambda t, c: (0, 0, t), pipeline_mode=pl.Buffered(1)),
            pl.BlockSpec((PEER_HEADS, 1, TMP), lambda t, c: (0, 0, t), pipeline_mode=pl.Buffered(1)),
            pl.BlockSpec((None, EC, D_MODEL), lambda t, c: (layer, c, 0)),
            pl.BlockSpec((None, None, D_MODEL, EC), lambda t, c: (layer, c, 0, 0)),
        ],
        out_specs=pl.BlockSpec((D_MODEL, TMP), lambda t, c: (0, t)),
        out_shape=jax.ShapeDtypeStruct((D_MODEL, n_tok), F32),
        scratch_shapes=([pltpu.VMEM((EC // PEER_SUB, TMP), F32)] * PEER_SUB
                        + [pltpu.VMEM((EC // PEER_SUB, TMP), BF16)] * PEER_SUB),
        compiler_params=_params(("arbitrary", "arbitrary")),
        name="peer",
    )(h2, b1, b2, tau, u_all, vt_all)


def _final_kernel(x_ref, ft_ref, mod_ref, g_ref, o_ref):
    x = x_ref[...] + mod_ref[0, 5:6, :] * ft_ref[...].T
    o_ref[...] = _rms_rows(x) * g_ref[...]


def _final(x1, ffn_t, mod, g_final, row0, n_rows, mod_row):
    off = row0 // TM
    return pl.pallas_call(
        _final_kernel,
        grid=(n_rows // TM,),
        in_specs=[
            pl.BlockSpec((TM, D_MODEL), lambda i: (off + i, 0)),
            pl.BlockSpec((D_MODEL, TM), lambda i: (0, off + i)),
            pl.BlockSpec((1, N_MOD, D_MODEL), lambda i: (mod_row(i), 0, 0)),
            pl.BlockSpec((1, D_MODEL), lambda i: (0, 0)),
        ],
        out_specs=pl.BlockSpec((TM, D_MODEL), lambda i: (i, 0)),
        out_shape=jax.ShapeDtypeStruct((n_rows, D_MODEL), F32),
        compiler_params=_params(("arbitrary",)),
        name="final",
    )(x1, ffn_t, mod, g_final)


def _rope_tables(n_tokens):
    rows = n_tokens // GRID_W
    t = jnp.arange(rows * GRID_W)
    pos = jnp.stack([t // GRID_W, t % GRID_W], axis=-1).astype(F32)
    n_freq = HEAD_DIM // 4
    inv = ROPE_THETA ** (-jnp.arange(n_freq, dtype=F32) / n_freq)
    ang = pos[:, :, None] * inv
    ang = jnp.broadcast_to(ang[:, :, None, :], (ang.shape[0], 2, 2, n_freq)).reshape(-1, HEAD_DIM)
    cos, sin = jnp.cos(ang), jnp.sin(ang)
    first = (jnp.arange(HEAD_DIM) % (2 * n_freq)) < n_freq
    return cos, jnp.where(first, -sin, 0.0), jnp.where(first, 0.0, sin)


def kernel(x_prompt, x_sample, cache_k, cache_v, c, c_ctx, w_ada, b_ada, g_norm1, g_norm2, w_in, q_gain, k_gain,
           w_s, b_s, conv_w, w_out, w_pq, sub_keys, u_tab, v_tab, g_final):
    batch, seq, _ = x_prompt.shape
    dec_batch, dec_seq, _ = x_sample.shape
    depth = w_in.shape[0]
    past = cache_k.shape[2]
    n_ctx = batch * seq
    n_tok = n_ctx + dec_batch * dec_seq
    dims = dict(batch=batch, seq=seq, dec_batch=dec_batch, dec_seq=dec_seq, n_ctx=n_ctx, n_tok=n_tok)
    assert seq % TM == 0 and dec_seq % TM == 0 and n_ctx % dec_seq == 0 and n_tok % TMP == 0
    assert 1 + dec_batch <= COND_ROWS

    x_parts = (x_prompt.reshape(n_ctx, D_MODEL), x_sample.reshape(dec_batch * dec_seq, D_MODEL), 0)
    cond = jnp.zeros((COND_ROWS, D_MODEL), F32).at[0].set(c_ctx).at[1:1 + dec_batch].set(c)
    mod_all = _ada(cond, w_ada, b_ada).reshape(depth, COND_ROWS, N_MOD, D_MODEL)
    cos, sa, sb = _rope_tables(dec_seq)
    w_qk_all = _query_key_weights(w_pq, sub_keys)
    vt_all = _value_blocks(v_tab)
    u_all = u_tab.astype(BF16)
    w_in_all = w_in.astype(BF16)
    w_out_all = w_out.astype(BF16)

    tps_lat = dec_seq // TM
    n_ctx_tiles = n_ctx // TM
    ffn_t = None
    mod_prev = None
    new_k, new_v = [], []
    for l in range(depth):
        mod = mod_all[l]
        ws_bf = w_s[l].astype(BF16)
        bs_b = jnp.broadcast_to(b_s[l][:, :, None], (CM_HEADS, CHUNK, CM_DIM))
        ck = cache_k[:, l].reshape(dec_batch, past, ATT_KV_W).astype(BF16)
        cvt = cache_v[:, l].reshape(dec_batch, past, ATT_KV_W).astype(BF16).transpose(0, 2, 1)

        outs = _inproj(x_parts, ffn_t, mod_prev, mod, g_norm1[l][None], (w_in_all, l), q_gain[l][None], k_gain[l][None],
                       cos, sa, sb, ws_bf, bs_b, dims)
        if ffn_t is not None:
            x_parts = (outs[0], outs[0], n_ctx_tiles)
            outs = outs[1:]
        q, kr, vb, kf, vf, cm, bg, zc = outs
        new_k.append(kf.reshape(batch, seq, N_KV, HEAD_DIM))
        new_v.append(vf.reshape(batch, seq, N_KV, HEAD_DIM))
        attn_ctx, attn_lat = _attention(q, kr, vb.T, ck, cvt, dims)
        x, h2, b1, b2, tau = _outproj(attn_ctx, attn_lat, cm, bg, zc, x_parts, mod, (w_out_all, l), conv_w[l],
                                      g_norm2[l][None], (w_qk_all, l), dims)
        x_parts = (x, x, n_ctx_tiles)
        ffn_t = _peer(h2, b1, b2, tau, u_all, vt_all, l, dims)
        mod_prev = mod

    gf = g_final[None]
    y_prompt = _final(x, ffn_t, mod_prev, gf, 0, n_ctx, lambda i: 0)
    y_sample = _final(x, ffn_t, mod_prev, gf, n_ctx, n_tok - n_ctx, lambda i: 1 + i // tps_lat)
    return (y_prompt.reshape(batch, seq, D_MODEL), y_sample.reshape(dec_batch, dec_seq, D_MODEL),
            jnp.stack(new_k, axis=1), jnp.stack(new_v, axis=1))
```

```python
import functools
import math

import jax
import jax.numpy as jnp
from jax import lax
from jax.experimental import pallas as pl
from jax.experimental.pallas import tpu as pltpu

F32 = jnp.float32
BF16 = jnp.bfloat16

D_MODEL = 2048
GRID_W = 64
N_HEADS = 8
N_KV = 2
HEAD_DIM = 128
GROUP = N_HEADS // N_KV
ATT_Q_W = N_HEADS * HEAD_DIM
ATT_KV_W = N_KV * HEAD_DIM
ROPE_THETA = 10000.0
CHUNK = 128
CM_HEADS = 4
CM_DIM = 128
CM_W = CM_HEADS * CM_DIM
CV_W = 512
D_IN = ATT_Q_W + 2 * ATT_KV_W + 2 * CM_W + 3 * CV_W
PEER_HEADS = 8
N_KEYS = 128
N_EXPERTS = N_KEYS * N_KEYS
TOPK = 16
PK_HALF = 128
N_GROUPS = 2 * PEER_HEADS
N_MOD = 6
EPS = 1e-6
NEG_INF = float("-inf")
LOG2E = math.log2(math.e)
Q_SCALE = HEAD_DIM ** -0.5 * LOG2E

LANES = 128
SUBLANES = 8
VMEM_LIMIT = 56 * 1024 * 1024

TM = 256
TQ = 256
TMP = 1024
EC = 1024
PEER_SUB = 2
GATE_ROWS = 64
ADA_NT = 1024
COND_ROWS = 16

O_Q = 0
O_K = O_Q + ATT_Q_W
O_V = O_K + ATT_KV_W
O_UCM = O_V + ATT_KV_W
O_VCM = O_UCM + CM_W
O_BCV = O_VCM + CM_W
O_CCV = O_BCV + CV_W
O_HCV = O_CCV + CV_W


def _dot(a, b):
    return jnp.dot(a, b, preferred_element_type=F32)


def _dot_nt(a, b):
    return lax.dot_general(a, b, (((1,), (1,)), ((), ())), preferred_element_type=F32)


def _split(x):
    hi = x.astype(BF16)
    lo = (x - hi.astype(F32)).astype(BF16)
    return hi, lo


def _rms_rows(x):
    return x * lax.rsqrt(jnp.mean(x * x, axis=-1, keepdims=True) + EPS)


def _params(sem):
    return pltpu.CompilerParams(dimension_semantics=sem, vmem_limit_bytes=VMEM_LIMIT)


def _ada_kernel(c_ref, w_ref, b_ref, o_ref):
    c = c_ref[...]
    a = c * jax.nn.sigmoid(c)
    a_hi, a_lo = _split(a)
    w_hi, w_lo = _split(w_ref[0])
    o_ref[0] = _dot(a_hi, w_hi) + _dot(a_lo, w_hi) + _dot(a_hi, w_lo) + b_ref[0]


def _ada(cond, w_ada, b_ada):
    depth, _, n_out = w_ada.shape
    return pl.pallas_call(
        _ada_kernel,
        grid=(depth, n_out // ADA_NT),
        in_specs=[
            pl.BlockSpec((COND_ROWS, D_MODEL), lambda l, j: (0, 0)),
            pl.BlockSpec((1, D_MODEL, ADA_NT), lambda l, j: (l, 0, j)),
            pl.BlockSpec((1, 1, ADA_NT), lambda l, j: (l, 0, j)),
        ],
        out_specs=pl.BlockSpec((1, COND_ROWS, ADA_NT), lambda l, j: (l, 0, j)),
        out_shape=jax.ShapeDtypeStruct((depth, COND_ROWS, n_out), F32),
        compiler_params=_params(("arbitrary", "arbitrary")),
        name="ada",
    )(cond, w_ada, b_ada.reshape(depth, 1, n_out))


def _qk_kernel(w_ref, k_ref, o_ref):
    w_hi, w_lo = _split(w_ref[0])
    k_hi, k_lo = _split(k_ref[0, 0])
    o_ref[0] = (_dot_nt(w_hi, k_hi) + _dot_nt(w_lo, k_hi) + _dot_nt(w_hi, k_lo)).astype(BF16)


def _query_key_weights(w_pq, sub_keys):
    depth = w_pq.shape[0]
    return pl.pallas_call(
        _qk_kernel,
        grid=(depth, N_GROUPS),
        in_specs=[
            pl.BlockSpec((1, D_MODEL, PK_HALF), lambda l, g: (l, 0, g)),
            pl.BlockSpec((1, 1, N_KEYS, PK_HALF), lambda l, g: (l, g, 0, 0)),
        ],
        out_specs=pl.BlockSpec((1, D_MODEL, N_KEYS), lambda l, g: (l, 0, g)),
        out_shape=jax.ShapeDtypeStruct((depth, D_MODEL, N_GROUPS * N_KEYS), BF16),
        compiler_params=_params(("arbitrary", "arbitrary")),
        name="qk_weights",
    )(w_pq, sub_keys.reshape(depth, N_GROUPS, N_KEYS, PK_HALF))


def _value_blocks_kernel(v_ref, o_ref):
    o_ref[...] = v_ref[...].T.astype(BF16)


def _value_blocks(v_tab):
    depth = v_tab.shape[0]
    n_chunks = N_EXPERTS // EC
    return pl.pallas_call(
        _value_blocks_kernel,
        grid=(depth, n_chunks),
        in_specs=[pl.BlockSpec((None, EC, D_MODEL), lambda l, c: (l, c, 0))],
        out_specs=pl.BlockSpec((None, None, D_MODEL, EC), lambda l, c: (l, c, 0, 0)),
        out_shape=jax.ShapeDtypeStruct((depth, n_chunks, D_MODEL, EC), BF16),
        compiler_params=_params(("arbitrary", "arbitrary")),
        name="value_blocks",
    )(v_tab)


def _inproj_kernel(*refs, has_ffn, n_ctx_tiles):
    refs = list(refs)
    xa_ref = refs.pop(0)
    xb_ref = refs.pop(0)
    if has_ffn:
        ft_ref = refs.pop(0)
        modp_ref = refs.pop(0)
    (mod_ref, g1_ref, w_ref, qg_ref, kg_ref, cos_ref, sa_ref, sb_ref, ws_ref, bs_ref) = refs[:10]
    outs = refs[10:]
    if has_ffn:
        xo_ref = outs.pop(0)
    q_ref, kr_ref, vb_ref, kf_ref, vf_ref, cm_ref, bg_ref, zc_ref = outs

    i = pl.program_id(0)
    is_lat = i >= n_ctx_tiles
    x = jnp.where(is_lat, xb_ref[...], xa_ref[...])
    if has_ffn:
        x = x + modp_ref[0, 5:6, :] * ft_ref[...].T
        xo_ref[...] = x
    sh1 = mod_ref[0, 0:1, :]
    sc1 = mod_ref[0, 1:2, :]
    h = _rms_rows(x) * g1_ref[...]
    hb = (h * (1.0 + sc1) + sh1).astype(BF16)

    cos = jnp.where(is_lat, cos_ref[...], 1.0)
    sa = jnp.where(is_lat, sa_ref[...], 0.0)
    sb = jnp.where(is_lat, sb_ref[...], 0.0)

    def rope(y):
        return y * cos + pltpu.roll(y, 96, 1) * sa + pltpu.roll(y, 32, 1) * sb

    zq = _dot(hb, w_ref[:, O_Q:O_K])
    for hd in range(N_HEADS):
        sl = slice(hd * HEAD_DIM, (hd + 1) * HEAD_DIM)
        y = _rms_rows(zq[:, sl]) * qg_ref[...]
        q_ref[:, sl] = (rope(y) * Q_SCALE).astype(BF16)

    zk = _dot(hb, w_ref[:, O_K:O_V])
    zv = _dot(hb, w_ref[:, O_V:O_UCM])
    vb_ref[...] = zv.astype(BF16)
    kn = []
    for hd in range(N_KV):
        sl = slice(hd * HEAD_DIM, (hd + 1) * HEAD_DIM)
        y = _rms_rows(zk[:, sl]) * kg_ref[...]
        kn.append(y)
        kr_ref[:, sl] = rope(y).astype(BF16)

    @pl.when(jnp.logical_not(is_lat))
    def _():
        for hd in range(N_KV):
            kf_ref[:, hd * HEAD_DIM:(hd + 1) * HEAD_DIM] = kn[hd]
        vf_ref[...] = zv

    zu = _dot(hb, w_ref[:, O_UCM:O_VCM])
    zvc = _dot(hb, w_ref[:, O_VCM:O_BCV])
    n_chunks = zu.shape[0] // CHUNK
    for c in range(CM_HEADS):
        sl = slice(c * CM_DIM, (c + 1) * CM_DIM)
        vh = _rms_rows(zvc[:, sl]).astype(BF16)
        for n in range(n_chunks):
            rows = slice(n * CHUNK, (n + 1) * CHUNK)
            mixed = _dot(ws_ref[c], vh[rows]) + bs_ref[c]
            cm_ref[rows, sl] = (zu[rows, sl] * mixed).astype(BF16)

    bg_ref[...] = _dot(hb, w_ref[:, O_BCV:O_CCV])
    zc_ref[...] = _dot(hb, w_ref[:, O_CCV:O_HCV]) * _dot(hb, w_ref[:, O_HCV:D_IN])


def _layer_spec(stacked, layer):
    return pl.BlockSpec((None,) + stacked.shape[1:], lambda i: (layer,) + (0,) * (stacked.ndim - 1),
                        pipeline_mode=pl.Buffered(1))


def _stream_specs(x_parts, n_ctx_tiles):
    _, _, lat_off = x_parts
    return [pl.BlockSpec((TM, D_MODEL), lambda i: (jnp.minimum(i, n_ctx_tiles - 1), 0)),
            pl.BlockSpec((TM, D_MODEL), lambda i: (jnp.maximum(i - n_ctx_tiles, 0) + lat_off, 0))]


def _inproj(x_parts, ffn_t, mod_prev, mod, g1, w_in, q_gain, k_gain, cos, sa, sb, w_s, b_s, dims):
    n_tok, n_ctx, tps_lat = dims["n_tok"], dims["n_ctx"], dims["dec_seq"] // TM
    n_ctx_tiles = n_ctx // TM
    has_ffn = ffn_t is not None

    def mod_row(i):
        return jnp.where(i < n_ctx_tiles, 0, 1 + jnp.maximum(i - n_ctx_tiles, 0) // tps_lat)

    def pos_blk(i):
        return jnp.maximum(i - n_ctx_tiles, 0) % tps_lat

    def ctx_blk(i):
        return jnp.minimum(i, n_ctx_tiles - 1)

    tok = lambda w: pl.BlockSpec((TM, w), lambda i: (i, 0))
    full = lambda a: pl.BlockSpec(a.shape, lambda i: (0,) * a.ndim)
    mod_spec = pl.BlockSpec((1, N_MOD, D_MODEL), lambda i: (mod_row(i), 0, 0))
    rope_spec = pl.BlockSpec((TM, HEAD_DIM), lambda i: (pos_blk(i), 0))

    args, in_specs = [x_parts[0], x_parts[1]], _stream_specs(x_parts, n_ctx_tiles)
    if has_ffn:
        args += [ffn_t, mod_prev]
        in_specs += [pl.BlockSpec((D_MODEL, TM), lambda i: (0, i)), mod_spec]
    args += [mod, g1, w_in[0], q_gain, k_gain, cos, sa, sb, w_s, b_s]
    in_specs += [mod_spec, full(g1), _layer_spec(*w_in), full(q_gain), full(k_gain),
                 rope_spec, rope_spec, rope_spec, full(w_s), full(b_s)]

    out_shape, out_specs = [], []
    if has_ffn:
        out_shape.append(jax.ShapeDtypeStruct((n_tok, D_MODEL), F32))
        out_specs.append(tok(D_MODEL))
    ctx_spec = pl.BlockSpec((TM, ATT_KV_W), lambda i: (ctx_blk(i), 0))
    out_shape += [
        jax.ShapeDtypeStruct((n_tok, ATT_Q_W), BF16),
        jax.ShapeDtypeStruct((n_tok, ATT_KV_W), BF16),
        jax.ShapeDtypeStruct((n_tok, ATT_KV_W), BF16),
        jax.ShapeDtypeStruct((n_ctx, ATT_KV_W), F32),
        jax.ShapeDtypeStruct((n_ctx, ATT_KV_W), F32),
        jax.ShapeDtypeStruct((n_tok, CM_W), BF16),
        jax.ShapeDtypeStruct((n_tok, CV_W), F32),
        jax.ShapeDtypeStruct((n_tok, CV_W), F32),
    ]
    out_specs += [tok(ATT_Q_W), tok(ATT_KV_W), tok(ATT_KV_W), ctx_spec, ctx_spec,
                  tok(CM_W), tok(CV_W), tok(CV_W)]
    return pl.pallas_call(
        functools.partial(_inproj_kernel, has_ffn=has_ffn, n_ctx_tiles=n_ctx_tiles),
        grid=(n_tok // TM,),
        in_specs=in_specs,
        out_specs=out_specs,
        out_shape=out_shape,
        compiler_params=_params(("arbitrary",)),
        name="inproj",
    )(*args)


def _attend(q_ref, parts, o_ref):
    t = q_ref.shape[0]
    for kv in range(N_KV):
        ksl = slice(kv * HEAD_DIM, (kv + 1) * HEAD_DIM)
        qs = jnp.concatenate(
            [q_ref[:, (kv * GROUP + g) * HEAD_DIM:(kv * GROUP + g + 1) * HEAD_DIM] for g in range(GROUP)], axis=0)
        scores = [_dot_nt(k_ref[:, ksl], qs) for k_ref, _ in parts]
        m = scores[0].max(axis=0, keepdims=True)
        for s in scores[1:]:
            m = jnp.maximum(m, s.max(axis=0, keepdims=True))
        den = None
        num = None
        for s, (_, vt_ref) in zip(scores, parts):
            p = jnp.exp2(s - m)
            ps = p.sum(axis=0, keepdims=True)
            pv = _dot(vt_ref[ksl, :], p.astype(BF16))
            den = ps if den is None else den + ps
            num = pv if num is None else num + pv
        o = (num / den).T
        for g in range(GROUP):
            hd = kv * GROUP + g
            o_ref[:, hd * HEAD_DIM:(hd + 1) * HEAD_DIM] = o[g * t:(g + 1) * t].astype(BF16)


def _attn_ctx_kernel(q_ref, k_ref, v_ref, o_ref):
    _attend(q_ref, [(k_ref, v_ref)], o_ref)


def _attn_lat_kernel(q_ref, k_ref, v_ref, ck_ref, cv_ref, o_ref):
    _attend(q_ref, [(k_ref, v_ref), (ck_ref, cv_ref)], o_ref)


def _attention(q, kr, vbt, ck, cvt, dims):
    n_tok, n_ctx = dims["n_tok"], dims["n_ctx"]
    batch, seq, dec_batch, dec_seq = dims["batch"], dims["seq"], dims["dec_batch"], dims["dec_seq"]
    past = ck.shape[1]
    attn_ctx = pl.pallas_call(
        _attn_ctx_kernel,
        grid=(batch,),
        in_specs=[
            pl.BlockSpec((seq, ATT_Q_W), lambda b: (b, 0)),
            pl.BlockSpec((seq, ATT_KV_W), lambda b: (b, 0)),
            pl.BlockSpec((ATT_KV_W, seq), lambda b: (0, b)),
        ],
        out_specs=pl.BlockSpec((seq, ATT_Q_W), lambda b: (b, 0)),
        out_shape=jax.ShapeDtypeStruct((n_ctx, ATT_Q_W), BF16),
        compiler_params=_params(("arbitrary",)),
        name="attn_ctx",
    )(q, kr, vbt)
    nq = dec_seq // TQ
    q_off = n_ctx // TQ
    kv_off = n_ctx // dec_seq
    attn_lat = pl.pallas_call(
        _attn_lat_kernel,
        grid=(dec_batch, nq),
        in_specs=[
            pl.BlockSpec((TQ, ATT_Q_W), lambda b, j: (q_off + b * nq + j, 0)),
            pl.BlockSpec((dec_seq, ATT_KV_W), lambda b, j: (kv_off + b, 0)),
            pl.BlockSpec((ATT_KV_W, dec_seq), lambda b, j: (0, kv_off + b)),
            pl.BlockSpec((None, past, ATT_KV_W), lambda b, j: (b, 0, 0)),
            pl.BlockSpec((None, ATT_KV_W, past), lambda b, j: (b, 0, 0)),
        ],
        out_specs=pl.BlockSpec((TQ, ATT_Q_W), lambda b, j: (b * nq + j, 0)),
        out_shape=jax.ShapeDtypeStruct((n_tok - n_ctx, ATT_Q_W), BF16),
        compiler_params=_params(("arbitrary", "arbitrary")),
        name="attn_lat",
    )(q, kr, vbt, ck, cvt)
    return attn_ctx, attn_lat


def _extract_top(vals, idxs, n):
    big = jnp.int32(1 << 30)
    maxima = []
    for _ in range(n):
        m = functools.reduce(jnp.maximum, vals).max(axis=0, keepdims=True)
        cand = [jnp.where(v == m, ix, big) for v, ix in zip(vals, idxs)]
        mi = functools.reduce(jnp.minimum, cand).min(axis=0, keepdims=True)
        vals = [jnp.where(ix == mi, NEG_INF, v) for v, ix in zip(vals, idxs)]
        maxima.append(m)
    return maxima, vals


def _oddeven_merge_sort_pairs(lo, hi):
    def merge(lo, hi, r):
        step = r * 2
        if step < hi - lo:
            yield from merge(lo, hi, step)
            yield from merge(lo + r, hi, step)
            for i in range(lo + r, hi - r, step):
                yield (i, i + r)
        else:
            yield (lo, lo + r)

    if hi - lo >= 1:
        mid = lo + (hi - lo) // 2
        yield from _oddeven_merge_sort_pairs(lo, mid)
        yield from _oddeven_merge_sort_pairs(mid + 1, hi)
        yield from merge(lo, hi, 1)


_SORT16 = tuple(_oddeven_merge_sort_pairs(0, TOPK - 1))


def _compare_exchange(v, i, j):
    if v[j] is None:
        return
    if v[i] is None:
        v[i], v[j] = v[j], None
        return
    v[i], v[j] = jnp.maximum(v[i], v[j]), jnp.minimum(v[i], v[j])


def _merge_sublanes(v):
    n = len(v)
    for shift in (4, 2, 1):
        y = [None if v[n - 1 - a] is None else pltpu.roll(v[n - 1 - a], shift, 0) for a in range(n)]
        v = [y[a] if v[a] is None else (v[a] if y[a] is None else jnp.maximum(v[a], y[a])) for a in range(n)]
        for d in (8, 4, 2, 1):
            for i in range(n):
                if not i & d:
                    _compare_exchange(v, i, i + d)
    return v


def _top16_sorted(s):
    blocks = [s[SUBLANES * a:SUBLANES * (a + 1), :] for a in range(N_KEYS // SUBLANES)]
    v = list(blocks)
    for i, j in _SORT16:
        _compare_exchange(v, i, j)
    v = _merge_sublanes(v)
    count = functools.reduce(jnp.add, [jnp.where(b >= v[TOPK - 1], 1.0, 0.0) for b in blocks]).sum(axis=0, keepdims=True)
    return v, count


def _stack_rows(v, r8):
    halves = []
    for base in (0, SUBLANES):
        acc = v[base]
        for k in range(1, SUBLANES):
            acc = jnp.where(r8 == k, v[base + k], acc)
        halves.append(acc)
    return jnp.concatenate(halves, axis=0)


def _kth_largest(vals):
    chain = [vals[0]] + list(vals[2:9])
    for extra in (vals[1], vals[9]):
        x = extra
        for k in range(len(chain)):
            chain[k], x = jnp.maximum(chain[k], x), jnp.minimum(chain[k], x)
        chain.append(x)
    v = _merge_sublanes(chain + [None] * (TOPK - len(chain)))
    return v[TOPK - 1][0:1]


def _pair_blocks(m1, m2, r8):
    vals = [m1[0:1] + m2[0:8], m1[0:1] + m2[8:16], m1[1:2] + m2[0:8]]
    idxs = [r8, r8 + 8, r8 + TOPK]
    for i in range(2, 8):
        vals.append(jnp.where(r8 < TOPK // (i + 1), m1[i:i + 1] + m2[0:8], NEG_INF))
        idxs.append(r8 + TOPK * i)
    vals.append(m1[8:16] + m2[0:1])
    idxs.append((r8 + 8) * TOPK)
    return vals, idxs


def _outproj_kernel(attn_c_ref, attn_l_ref, cm_ref, bg_ref, zc_ref, zp_ref, zn_ref, xa_ref, xb_ref, mod_ref, wo_ref, cw_ref, g2_ref,
                    wk_ref,
                    x1_ref, h2_ref, b1_ref, b2_ref, tau_ref,
                    zpad_scr, st_scr, m_scr, *, n_ctx_tiles, tps_ctx, tps_lat):
    i = pl.program_id(0)
    tm = xa_ref.shape[0]
    is_lat = i >= n_ctx_tiles
    jj = jnp.where(is_lat, jnp.maximum(i - n_ctx_tiles, 0) % tps_lat, i % tps_ctx)
    nt = jnp.where(is_lat, tps_lat, tps_ctx)

    zc = zc_ref[...]
    zpad_scr[0:SUBLANES, :] = jnp.where(jj == 0, 0.0, zp_ref[...])
    zpad_scr[SUBLANES:SUBLANES + tm, :] = zc
    zpad_scr[SUBLANES + tm:2 * SUBLANES + tm, :] = jnp.where(jj == nt - 1, 0.0, zn_ref[...])
    zprev = zpad_scr[SUBLANES - 1:SUBLANES - 1 + tm, :]
    znext = zpad_scr[SUBLANES + 1:SUBLANES + 1 + tm, :]
    y = zprev * cw_ref[0:1, :] + zc * cw_ref[1:2, :] + znext * cw_ref[2:3, :]
    cv = (bg_ref[...] * y).astype(BF16)

    attn = jnp.where(is_lat, attn_l_ref[...], attn_c_ref[...])
    mix = _dot(jnp.concatenate([attn, cm_ref[...], cv], axis=1), wo_ref[...])
    g1 = mod_ref[0, 2:3, :]
    sh2 = mod_ref[0, 3:4, :]
    sc2 = mod_ref[0, 4:5, :]
    x1 = jnp.where(is_lat, xb_ref[...], xa_ref[...]) + g1 * mix
    x1_ref[...] = x1
    h2 = (_rms_rows(x1) * g2_ref[...]) * (1.0 + sc2) + sh2
    hb = h2.astype(BF16)
    h2_ref[...] = hb

    scores = _dot(hb, wk_ref[...])
    for g in range(N_GROUPS):
        st_scr[g] = scores[:, g * N_KEYS:(g + 1) * N_KEYS].T

    n_lt = tm // LANES
    r8 = lax.broadcasted_iota(jnp.int32, (SUBLANES, LANES), 0)

    def finish(h, lt, lanes, a1, a2, top, picked, vals, m1, m2):
        z = functools.reduce(
            jnp.add, [jnp.where(p, jnp.exp(v - top), 0.0) for p, v in zip(picked, vals)]).sum(axis=0, keepdims=True)
        lz = jnp.log(z)
        shifted = _pair_blocks(((m1 - top) - lz) * LOG2E - 1.0, m2 * LOG2E, r8)[0]
        tau = functools.reduce(
            jnp.minimum, [jnp.where(p, v, jnp.inf) for p, v in zip(picked, shifted)]).min(axis=0, keepdims=True)
        tau_ref[h, :, lanes] = tau
        b1_ref[h, lt] = ((a1 - top) - lz) * LOG2E - 1.0
        b2_ref[h, :, lanes] = a2 * LOG2E

    def fast(h, bad):
        for lt in range(n_lt):
            lanes = slice(lt * LANES, (lt + 1) * LANES)
            s1 = st_scr[2 * h, :, lanes]
            s2 = st_scr[2 * h + 1, :, lanes]
            v1, n1 = _top16_sorted(s1)
            v2, n2 = _top16_sorted(s2)
            m1 = _stack_rows(v1, r8)
            m2 = _stack_rows(v2, r8)
            top = m1[0:1] + m2[0:1]
            vals = _pair_blocks(m1, m2, r8)[0]
            thr = _kth_largest(vals)
            picked = [v >= thr for v in vals]
            n12 = functools.reduce(jnp.add, [jnp.where(p, 1.0, 0.0) for p in picked]).sum(axis=0, keepdims=True)
            bad = jnp.maximum(bad, jnp.where((n1 == TOPK) & (n2 == TOPK) & (n12 == TOPK), 0.0, 1.0))
            a1 = jnp.where(s1 >= v1[TOPK - 1][0:1], s1, NEG_INF)
            a2 = jnp.where(s2 >= v2[TOPK - 1][0:1], s2, NEG_INF)
            finish(h, lt, lanes, a1, a2, top, picked, vals, m1, m2)
        return bad

    bad = lax.fori_loop(0, PEER_HEADS, fast, jnp.zeros((1, LANES), F32))

    @pl.when(jnp.max(bad) > 0.0)
    def _():
        def stage1(g, carry):
            for lt in range(n_lt):
                lanes = slice(lt * LANES, (lt + 1) * LANES)
                s = st_scr[g, :, lanes]
                rows = lax.broadcasted_iota(jnp.int32, s.shape, 0)
                maxima, left = _extract_top([s], [rows], TOPK)
                m_scr[g, :, lanes] = jnp.concatenate(maxima, axis=0)
                st_scr[g, :, lanes] = jnp.where(left[0] == s, NEG_INF, s)
            return carry

        lax.fori_loop(0, N_GROUPS, stage1, 0)

        def stage2(h, carry):
            for lt in range(n_lt):
                lanes = slice(lt * LANES, (lt + 1) * LANES)
                m1 = m_scr[2 * h, :, lanes]
                m2 = m_scr[2 * h + 1, :, lanes]
                top = m1[0:1] + m2[0:1]
                vals, idxs = _pair_blocks(m1, m2, r8)
                _, left = _extract_top(vals, idxs, TOPK)
                picked = [l != v for l, v in zip(left, vals)]
                finish(h, lt, lanes, st_scr[2 * h, :, lanes], st_scr[2 * h + 1, :, lanes], top, picked, vals, m1, m2)
            return carry

        lax.fori_loop(0, PEER_HEADS, stage2, 0)


def _outproj(attn_ctx, attn_lat, cm, bg, zc, x_parts, mod, w_out, conv_w, g2, w_qk, dims):
    n_tok, n_ctx = dims["n_tok"], dims["n_ctx"]
    n_ctx_tiles = n_ctx // TM
    n_tiles = n_tok // TM
    tps_lat = dims["dec_seq"] // TM
    tps_ctx = dims["seq"] // TM
    halo_per_tile = TM // SUBLANES
    n_halo = n_tok // SUBLANES

    def mod_row(i):
        return jnp.where(i < n_ctx_tiles, 0, 1 + jnp.maximum(i - n_ctx_tiles, 0) // tps_lat)

    tok = lambda w: pl.BlockSpec((TM, w), lambda i: (i, 0))
    full = lambda a: pl.BlockSpec(a.shape, lambda i: (0,) * a.ndim)
    sel_spec = pl.BlockSpec((PEER_HEADS, N_KEYS, TM), lambda i: (0, 0, i))
    return pl.pallas_call(
        functools.partial(_outproj_kernel, n_ctx_tiles=n_ctx_tiles, tps_ctx=tps_ctx, tps_lat=tps_lat),
        grid=(n_tiles,),
        in_specs=[
            pl.BlockSpec((TM, ATT_Q_W), lambda i: (jnp.minimum(i, n_ctx_tiles - 1), 0)),
            pl.BlockSpec((TM, ATT_Q_W), lambda i: (jnp.maximum(i - n_ctx_tiles, 0), 0)),
            tok(CM_W), tok(CV_W), tok(CV_W),
            pl.BlockSpec((SUBLANES, CV_W), lambda i: (jnp.maximum(i * halo_per_tile - 1, 0), 0)),
            pl.BlockSpec((SUBLANES, CV_W), lambda i: (jnp.minimum((i + 1) * halo_per_tile, n_halo - 1), 0)),
            *_stream_specs(x_parts, n_ctx_tiles),
            pl.BlockSpec((1, N_MOD, D_MODEL), lambda i: (mod_row(i), 0, 0)),
            _layer_spec(*w_out), full(conv_w), full(g2), _layer_spec(*w_qk),
        ],
        out_specs=[
            tok(D_MODEL), tok(D_MODEL),
            pl.BlockSpec((PEER_HEADS, TM // LANES, N_KEYS, LANES), lambda i: (0, i, 0, 0)),
            sel_spec,
            pl.BlockSpec((PEER_HEADS, 1, TM), lambda i: (0, 0, i)),
        ],
        out_shape=[
            jax.ShapeDtypeStruct((n_tok, D_MODEL), F32),
            jax.ShapeDtypeStruct((n_tok, D_MODEL), BF16),
            jax.ShapeDtypeStruct((PEER_HEADS, n_tok // LANES, N_KEYS, LANES), F32),
            jax.ShapeDtypeStruct((PEER_HEADS, N_KEYS, n_tok), F32),
            jax.ShapeDtypeStruct((PEER_HEADS, 1, n_tok), F32),
        ],
        scratch_shapes=[
            pltpu.VMEM((TM + 2 * SUBLANES, CV_W), F32),
            pltpu.VMEM((N_GROUPS, N_KEYS, TM), F32),
            pltpu.VMEM((N_GROUPS, TOPK, TM), F32),
        ],
        compiler_params=_params(("arbitrary",)),
        name="outproj",
    )(attn_ctx, attn_lat, cm, bg, zc, zc, zc, x_parts[0], x_parts[1], mod, w_out[0], conv_w, g2, w_qk[0])


GELU_C0 = math.sqrt(2.0 / math.pi)
GELU_C1 = GELU_C0 * 0.044715


def _gated_gelu(half_gate, x):
    hx = half_gate * x
    return hx * jnp.tanh(x * (GELU_C0 + GELU_C1 * (x * x))) + hx


def _peer_kernel(h2_ref, b1_ref, b2_ref, tau_ref, u_ref, vt_ref, o_ref, *scratch):
    a_scrs, w_scrs = scratch[:PEER_SUB], scratch[PEER_SUB:]
    c = pl.program_id(1)
    tmp = h2_ref.shape[0]
    ec = u_ref.shape[0]
    keys_per_step = ec // N_KEYS

    n_sub = len(a_scrs)
    es = ec // n_sub
    keys_per_sub = es // N_KEYS

    def scores(s):
        a_scrs[s][...] = _dot_nt(u_ref[s * es:(s + 1) * es, :], h2_ref[...])

    def mix(s):
        o_ref[...] += _dot(vt_ref[:, s * es:(s + 1) * es], w_scrs[s][...])

    def gates(s):
        for j in range(keys_per_sub):
            i1 = c * keys_per_step + s * keys_per_sub + j
            for tb in range(tmp // LANES):
                tl = slice(tb * LANES, (tb + 1) * LANES)
                for half in range(N_KEYS // GATE_ROWS):
                    keys = slice(half * GATE_ROWS, (half + 1) * GATE_ROWS)
                    rows = slice(j * N_KEYS + half * GATE_ROWS, j * N_KEYS + (half + 1) * GATE_ROWS)
                    gate = None
                    for h in range(PEER_HEADS):
                        row = b1_ref[h, tb, pl.ds(i1, SUBLANES, stride=0), :]
                        row = jnp.broadcast_to(row[None], (GATE_ROWS // SUBLANES, SUBLANES, LANES)).reshape(
                            GATE_ROWS, LANES)
                        t = row + b2_ref[h, keys, tl]
                        g = jnp.where(t >= tau_ref[h, :, tl], jnp.exp2(t), 0.0)
                        gate = g if gate is None else gate + g
                    w_scrs[s][rows, tl] = _gated_gelu(gate, a_scrs[s][rows, tl]).astype(BF16)

    @pl.when(c == 0)
    def _():
        o_ref[...] = jnp.zeros_like(o_ref)

    scores(0)
    for s in range(n_sub):
        if s + 1 < n_sub:
            scores(s + 1)
        gates(s)
        mix(s)


def _peer(h2, b1, b2, tau, u_all, vt_all, layer, dims):
    n_tok = dims["n_tok"]
    n_chunks = N_EXPERTS // EC
    return pl.pallas_call(
        _peer_kernel,
        grid=(n_tok // TMP, n_chunks),
        in_specs=[
            pl.BlockSpec((TMP, D_MODEL), lambda t, c: (t, 0), pipeline_mode=pl.Buffered(1)),
            pl.BlockSpec((PEER_HEADS, TMP // LANES, N_KEYS, LANES), lambda t, c: (0, t, 0, 0),
                         pipeline_mode=pl.Buffered(1)),
            pl.BlockSpec((PEER_HEADS, N_KEYS, TMP), lambda t, c: (0, 0, t), pipeline_mode=pl.Buffered(1)),
            pl.BlockSpec((PEER_HEADS, 1, TMP), lambda t, c: (0, 0, t), pipeline_mode=pl.Buffered(1)),
            pl.BlockSpec((None, EC, D_MODEL), lambda t, c: (layer, c, 0)),
            pl.BlockSpec((None, None, D_MODEL, EC), lambda t, c: (layer, c, 0, 0)),
        ],
        out_specs=pl.BlockSpec((D_MODEL, TMP), lambda t, c: (0, t)),
        out_shape=jax.ShapeDtypeStruct((D_MODEL, n_tok), F32),
        scratch_shapes=([pltpu.VMEM((EC // PEER_SUB, TMP), F32)] * PEER_SUB
                        + [pltpu.VMEM((EC // PEER_SUB, TMP), BF16)] * PEER_SUB),
        compiler_params=_params(("arbitrary", "arbitrary")),
        name="peer",
    )(h2, b1, b2, tau, u_all, vt_all)


def _final_kernel(x_ref, ft_ref, mod_ref, g_ref, o_ref):
    x = x_ref[...] + mod_ref[0, 5:6, :] * ft_ref[...].T
    o_ref[...] = _rms_rows(x) * g_ref[...]


def _final(x1, ffn_t, mod, g_final, row0, n_rows, mod_row):
    off = row0 // TM
    return pl.pallas_call(
        _final_kernel,
        grid=(n_rows // TM,),
        in_specs=[
            pl.BlockSpec((TM, D_MODEL), lambda i: (off + i, 0)),
            pl.BlockSpec((D_MODEL, TM), lambda i: (0, off + i)),
            pl.BlockSpec((1, N_MOD, D_MODEL), lambda i: (mod_row(i), 0, 0)),
            pl.BlockSpec((1, D_MODEL), lambda i: (0, 0)),
        ],
        out_specs=pl.BlockSpec((TM, D_MODEL), lambda i: (i, 0)),
        out_shape=jax.ShapeDtypeStruct((n_rows, D_MODEL), F32),
        compiler_params=_params(("arbitrary",)),
        name="final",
    )(x1, ffn_t, mod, g_final)


def _rope_tables(n_tokens):
    rows = n_tokens // GRID_W
    t = jnp.arange(rows * GRID_W)
    pos = jnp.stack([t // GRID_W, t % GRID_W], axis=-1).astype(F32)
    n_freq = HEAD_DIM // 4
    inv = ROPE_THETA ** (-jnp.arange(n_freq, dtype=F32) / n_freq)
    ang = pos[:, :, None] * inv
    ang = jnp.broadcast_to(ang[:, :, None, :], (ang.shape[0], 2, 2, n_freq)).reshape(-1, HEAD_DIM)
    cos, sin = jnp.cos(ang), jnp.sin(ang)
    first = (jnp.arange(HEAD_DIM) % (2 * n_freq)) < n_freq
    return cos, jnp.where(first, -sin, 0.0), jnp.where(first, 0.0, sin)


def kernel(x_prompt, x_sample, cache_k, cache_v, c, c_ctx, w_ada, b_ada, g_norm1, g_norm2, w_in, q_gain, k_gain,
           w_s, b_s, conv_w, w_out, w_pq, sub_keys, u_tab, v_tab, g_final):
    batch, seq, _ = x_prompt.shape
    dec_batch, dec_seq, _ = x_sample.shape
    depth = w_in.shape[0]
    past = cache_k.shape[2]
    n_ctx = batch * seq
    n_tok = n_ctx + dec_batch * dec_seq
    dims = dict(batch=batch, seq=seq, dec_batch=dec_batch, dec_seq=dec_seq, n_ctx=n_ctx, n_tok=n_tok)
    assert seq % TM == 0 and dec_seq % TM == 0 and n_ctx % dec_seq == 0 and n_tok % TMP == 0
    assert 1 + dec_batch <= COND_ROWS

    x_parts = (x_prompt.reshape(n_ctx, D_MODEL), x_sample.reshape(dec_batch * dec_seq, D_MODEL), 0)
    cond = jnp.zeros((COND_ROWS, D_MODEL), F32).at[0].set(c_ctx).at[1:1 + dec_batch].set(c)
    mod_all = _ada(cond, w_ada, b_ada).reshape(depth, COND_ROWS, N_MOD, D_MODEL)
    cos, sa, sb = _rope_tables(dec_seq)
    w_qk_all = _query_key_weights(w_pq, sub_keys)
    vt_all = _value_blocks(v_tab)
    u_all = u_tab.astype(BF16)
    w_in_all = w_in.astype(BF16)
    w_out_all = w_out.astype(BF16)

    tps_lat = dec_seq // TM
    n_ctx_tiles = n_ctx // TM
    ffn_t = None
    mod_prev = None
    new_k, new_v = [], []
    for l in range(depth):
        mod = mod_all[l]
        ws_bf = w_s[l].astype(BF16)
        bs_b = jnp.broadcast_to(b_s[l][:, :, None], (CM_HEADS, CHUNK, CM_DIM))
        ck = cache_k[:, l].reshape(dec_batch, past, ATT_KV_W).astype(BF16)
        cvt = cache_v[:, l].reshape(dec_batch, past, ATT_KV_W).astype(BF16).transpose(0, 2, 1)

        outs = _inproj(x_parts, ffn_t, mod_prev, mod, g_norm1[l][None], (w_in_all, l), q_gain[l][None], k_gain[l][None],
                       cos, sa, sb, ws_bf, bs_b, dims)
        if ffn_t is not None:
            x_parts = (outs[0], outs[0], n_ctx_tiles)
            outs = outs[1:]
        q, kr, vb, kf, vf, cm, bg, zc = outs
        new_k.append(kf.reshape(batch, seq, N_KV, HEAD_DIM))
        new_v.append(vf.reshape(batch, seq, N_KV, HEAD_DIM))
        attn_ctx, attn_lat = _attention(q, kr, vb.T, ck, cvt, dims)
        x, h2, b1, b2, tau = _outproj(attn_ctx, attn_lat, cm, bg, zc, x_parts, mod, (w_out_all, l), conv_w[l],
                                      g_norm2[l][None], (w_qk_all, l), dims)
        x_parts = (x, x, n_ctx_tiles)
        ffn_t = _peer(h2, b1, b2, tau, u_all, vt_all, l, dims)
        mod_prev = mod

    gf = g_final[None]
    y_prompt = _final(x, ffn_t, mod_prev, gf, 0, n_ctx, lambda i: 0)
    y_sample = _final(x, ffn_t, mod_prev, gf, n_ctx, n_tok - n_ctx, lambda i: 1 + i // tps_lat)
    return (y_prompt.reshape(batch, seq, D_MODEL), y_sample.reshape(dec_batch, dec_seq, D_MODEL),
            jnp.stack(new_k, axis=1), jnp.stack(new_v, axis=1))
```
